```python
import jax
import jax.numpy as jnp
from jax import lax
import numpy as np

D_MODEL = 1024
BATCH = 2
SEQ = 16384
DEPTH = 4
DEC_BATCH = 8
DEC_SEQ = 16
PAST_LEN = 1024

CHUNK = 64
N_MIXERS = 2
N_RWKV = (DEPTH + 1) // 2
N_MLSTM = DEPTH // 2
RW_HEAD = 64
RW_HEADS = D_MODEL // RW_HEAD
LORA_W = 64
LORA_A = 64
LORA_V = 32
LORA_G = 128
GN_EPS = 64e-5
ML_HEADS = 4
ML_DQK = D_MODEL // (2 * ML_HEADS)
ML_DV = D_MODEL // ML_HEADS
GATE_CAP = 15.0
ML_IN = 2 * ML_HEADS * ML_DQK + 2 * ML_HEADS * ML_DV + 2 * ML_HEADS
D_FF = ((8 * D_MODEL + 3 * 256 - 1) // (3 * 256)) * 256
EPS = 1e-6

kernel_name = 'rwkv7_mlstm_streaming_step'


def _rmsnorm(x, g):
    xf = x.astype(jnp.float32)
    y = xf * lax.rsqrt(jnp.mean(xf * xf, axis=-1, keepdims=True) + EPS)
    return (y * g.astype(jnp.float32)).astype(x.dtype)


def _swiglu(x, w_gu, w_down):
    gu = x @ w_gu
    return (jax.nn.silu(gu[..., :D_FF]) * gu[..., D_FF:]) @ w_down


def _rwkv7_time_mix(x, shift0, S0, v_first, p, j):
    f32 = jnp.float32
    B, T, _ = x.shape
    H, N = RW_HEADS, RW_HEAD
    P = lambda name, idx=j: p[name][idx].astype(f32)
    xf = x.astype(f32)
    prev = jnp.concatenate([shift0[:, None, :].astype(f32), xf[:, :-1]], axis=1)
    dx = prev - xf
    mu = P('rw_mu')
    xr, xw, xk, xv, xa, xg = (xf + dx * mu[c] for c in range(6))
    w_rkv = P('rw_w_rkv')
    r = xr @ w_rkv[0]
    k = xk @ w_rkv[1]
    v = xv @ w_rkv[2]
    wl = P('rw_w0') + jnp.tanh(xw @ P('rw_w1')) @ P('rw_w2')
    decay = jnp.exp(-jnp.exp(-jax.nn.softplus(-wl) - 0.5))
    a = jax.nn.sigmoid(P('rw_a0') + (xa @ P('rw_a1')) @ P('rw_a2'))
    g = jax.nn.sigmoid(xg @ P('rw_g1')) @ P('rw_g2')
    if j == 0:
        v_first = v
    else:
        v = v + (v_first - v) * jax.nn.sigmoid(P('rw_v0', j - 1) + (xv @ P('rw_v1', j - 1)) @ P('rw_v2', j - 1))
    kk = (k * P('rw_k_k')).reshape(B, T, H, N)
    kk = kk * lax.rsqrt(jnp.maximum(jnp.sum(kk * kk, axis=-1, keepdims=True), 1e-24))
    k = k * (1.0 + (a - 1.0) * P('rw_k_a'))
    hv = lambda z: z.reshape(B, T, H, N)
    rh, wh, kh, vh, ah = hv(r), hv(decay), hv(k), hv(v), hv(a)

    def step(S, inp):
        r_t, w_t, k_t, v_t, kk_t, a_t = inp
        sa = jnp.einsum('bhvk,bhk->bhv', S, -kk_t)
        S = S * w_t[:, :, None, :] + sa[..., None] * (kk_t * a_t)[:, :, None, :] + v_t[..., None] * k_t[:, :, None, :]
        return S, jnp.einsum('bhvk,bhk->bhv', S, r_t)

    xs = tuple(jnp.moveaxis(z, 1, 0) for z in (rh, wh, kh, vh, kk, ah))
    S_T, o = lax.scan(step, S0.astype(f32), xs)
    o = jnp.moveaxis(o, 0, 1)
    mean = jnp.mean(o, axis=-1, keepdims=True)
    var = jnp.mean(jnp.square(o - mean), axis=-1, keepdims=True)
    o = ((o - mean) * lax.rsqrt(var + GN_EPS)).reshape(B, T, D_MODEL) * P('rw_ln_w') + P('rw_ln_b')
    bonus = jnp.sum(rh * kh * P('rw_r_k'), axis=-1, keepdims=True) * vh
    o = o + bonus.reshape(B, T, D_MODEL)
    y = (o * g) @ P('rw_w_o')
    return y.astype(x.dtype), xf[:, -1], S_T, v_first


def _mlstm_mix(x, C0, n0, m0, p, j):
    f32 = jnp.float32
    B, T, _ = x.shape
    H = ML_HEADS
    xf = x.astype(f32)
    z = xf @ p['ml_w_in'][j].astype(f32)
    nq, nv = H * ML_DQK, H * ML_DV
    q = z[..., :nq].reshape(B, T, H, ML_DQK) * (ML_DQK ** -0.5)
    k = z[..., nq:2 * nq].reshape(B, T, H, ML_DQK)
    v = z[..., 2 * nq:2 * nq + nv].reshape(B, T, H, ML_DV)
    o_gate = jax.nn.sigmoid(z[..., 2 * nq + nv:2 * nq + 2 * nv])
    gates = z[..., 2 * nq + 2 * nv:] + p['ml_b_gates'][j].astype(f32)
    gates = GATE_CAP * jnp.tanh(gates / GATE_CAP)
    li = gates[..., :H]
    lf = jax.nn.log_sigmoid(gates[..., H:])
    L = min(CHUNK, T)
    nc = T // L

    def to_chunks(t):
        t = t.reshape((B, nc, L) + t.shape[2:])
        return jnp.moveaxis(t, (1, 3), (0, 2))

    causal = jnp.tril(jnp.ones((L, L), dtype=bool))

    def chunk_step(carry, inp):
        C, n, m = carry
        q_c, k_c, v_c, li_c, lf_c = inp
        b = jnp.cumsum(lf_c, axis=-1)
        dmat = jnp.where(causal, b[..., :, None] - b[..., None, :] + li_c[..., None, :], -jnp.inf)
        inter = b + m[..., None]
        m_t = jnp.maximum(inter, jnp.max(dmat, axis=-1))
        wts = jnp.exp(dmat - m_t[..., None])
        sc = jnp.exp(inter - m_t)
        qk = jnp.einsum('bhtd,bhsd->bhts', q_c, k_c) * wts
        num = jnp.einsum('bhts,bhsv->bhtv', qk, v_c) + sc[..., None] * jnp.einsum('bhtd,bhdv->bhtv', q_c, C)
        den = jnp.sum(qk, axis=-1) + sc * jnp.einsum('bhtd,bhd->bht', q_c, n)
        h = num / jnp.maximum(jnp.abs(den), jnp.exp(-m_t))[..., None]
        m_new = m_t[..., -1]
        w_end = jnp.exp(b[..., -1:] - b + li_c - m_new[..., None])
        sc_end = jnp.exp(b[..., -1] + m - m_new)
        C = sc_end[..., None, None] * C + jnp.einsum('bhs,bhsd,bhsv->bhdv', w_end, k_c, v_c)
        n = sc_end[..., None] * n + jnp.einsum('bhs,bhsd->bhd', w_end, k_c)
        return (C, n, m_new), h

    xs = (to_chunks(q), to_chunks(k), to_chunks(v), to_chunks(li), to_chunks(lf))
    (C_T, n_T, m_T), h = lax.scan(chunk_step, (C0.astype(f32), n0.astype(f32), m0.astype(f32)), xs)
    h = jnp.moveaxis(h, (0, 2), (1, 3)).reshape(B, T, H, ML_DV)
    h = h * lax.rsqrt(jnp.mean(h * h, axis=-1, keepdims=True) + EPS)
    h = h.reshape(B, T, D_MODEL) * p['ml_hn_w'][j].astype(f32)
    y = (o_gate * h) @ p['ml_w_out'][j].astype(f32)
    return y.astype(x.dtype), C_T, n_T, m_T


def _trunk(x, rw_shift, rw_S, ml_C, ml_n, ml_m, p):
    v_first = None
    shifts, Ss, Cs, ns, ms = [], [], [], [], []
    for i in range(DEPTH):
        g = p['norm_g'][i]
        j = i // N_MIXERS
        h = _rmsnorm(x, g[0])
        if i % N_MIXERS == 0:
            y, sh, S, v_first = _rwkv7_time_mix(h, rw_shift[j], rw_S[j], v_first, p, j)
            shifts.append(sh)
            Ss.append(S)
        else:
            y, C, n, m = _mlstm_mix(h, ml_C[j], ml_n[j], ml_m[j], p, j)
            Cs.append(C)
            ns.append(n)
            ms.append(m)
        x = x + _rmsnorm(y, g[1])
        h = _rmsnorm(x, g[2])
        x = x + _rmsnorm(_swiglu(h, p['ffn_w_gu'][i], p['ffn_w_down'][i]), g[3])
    return x, jnp.stack(shifts), jnp.stack(Ss), jnp.stack(Cs), jnp.stack(ns), jnp.stack(ms)


def setup_inputs(seed: int = 0) -> dict:
    key = jax.random.key(seed)
    ks = jax.random.split(key, 40)
    f32 = jnp.float32
    D = D_MODEL

    def nrm(i, shape, scale):
        return scale * jax.random.normal(ks[i], shape, f32)

    nvr = max(N_RWKV - 1, 0)
    b_i = -2.0 + nrm(28, (N_MLSTM, ML_HEADS), 0.1)
    b_f = jnp.linspace(3.0, 6.0, ML_HEADS, dtype=f32)[None, :] + nrm(29, (N_MLSTM, ML_HEADS), 0.1)
    return {
        'x_prompt': nrm(0, (BATCH, SEQ, D), 1.0),
        'x_sample': nrm(1, (DEC_BATCH, DEC_SEQ, D), 1.0),
        'state_rwkv_shift': nrm(2, (N_RWKV, DEC_BATCH, D), 1.0),
        'state_rwkv_S': nrm(3, (N_RWKV, DEC_BATCH, RW_HEADS, RW_HEAD, RW_HEAD), 0.3),
        'state_mlstm_C': nrm(4, (N_MLSTM, DEC_BATCH, ML_HEADS, ML_DQK, ML_DV), 0.1),
        'state_mlstm_n': nrm(5, (N_MLSTM, DEC_BATCH, ML_HEADS, ML_DQK), 0.1),
        'state_mlstm_m': nrm(6, (N_MLSTM, DEC_BATCH, ML_HEADS), 1.0),
        'norm_g': 1.0 + nrm(7, (DEPTH, 4, D), 0.05),
        'rw_mu': jax.random.uniform(ks[8], (N_RWKV, 6, D), f32),
        'rw_w_rkv': nrm(9, (N_RWKV, 3, D, D), D ** -0.5),
        'rw_w0': jnp.linspace(-6.0, -1.0, D, dtype=f32)[None, :] + nrm(10, (N_RWKV, D), 0.1),
        'rw_w1': nrm(11, (N_RWKV, D, LORA_W), D ** -0.5),
        'rw_w2': nrm(12, (N_RWKV, LORA_W, D), 0.1 * LORA_W ** -0.5),
        'rw_a0': nrm(13, (N_RWKV, D), 0.1),
        'rw_a1': nrm(14, (N_RWKV, D, LORA_A), D ** -0.5),
        'rw_a2': nrm(15, (N_RWKV, LORA_A, D), LORA_A ** -0.5),
        'rw_v0': nrm(16, (nvr, D), 0.1),
        'rw_v1': nrm(17, (nvr, D, LORA_V), D ** -0.5),
        'rw_v2': nrm(18, (nvr, LORA_V, D), LORA_V ** -0.5),
        'rw_g1': nrm(19, (N_RWKV, D, LORA_G), D ** -0.5),
        'rw_g2': nrm(20, (N_RWKV, LORA_G, D), LORA_G ** -0.5),
        'rw_k_k': 0.85 + nrm(21, (N_RWKV, D), 0.05),
        'rw_k_a': 1.0 + nrm(22, (N_RWKV, D), 0.05),
        'rw_r_k': nrm(23, (N_RWKV, RW_HEADS, RW_HEAD), 0.1),
        'rw_ln_w': 1.0 + nrm(24, (N_RWKV, D), 0.05),
        'rw_ln_b': nrm(25, (N_RWKV, D), 0.01),
        'rw_w_o': nrm(26, (N_RWKV, D, D), D ** -0.5),
        'ml_w_in': nrm(27, (N_MLSTM, D, ML_IN), D ** -0.5),
        'ml_b_gates': jnp.concatenate([b_i, b_f], axis=-1),
        'ml_hn_w': 1.0 + nrm(30, (N_MLSTM, D), 0.05),
        'ml_w_out': nrm(31, (N_MLSTM, D, D), D ** -0.5),
        'ffn_w_gu': nrm(32, (DEPTH, D, 2 * D_FF), D ** -0.5),
        'ffn_w_down': nrm(33, (DEPTH, D_FF, D), D_FF ** -0.5),
    }


def reference(x_prompt, x_sample, state_rwkv_shift, state_rwkv_S, state_mlstm_C, state_mlstm_n, state_mlstm_m,
              norm_g, rw_mu, rw_w_rkv, rw_w0, rw_w1, rw_w2, rw_a0, rw_a1, rw_a2, rw_v0, rw_v1, rw_v2,
              rw_g1, rw_g2, rw_k_k, rw_k_a, rw_r_k, rw_ln_w, rw_ln_b, rw_w_o,
              ml_w_in, ml_b_gates, ml_hn_w, ml_w_out, ffn_w_gu, ffn_w_down):
    p = {
        'norm_g': norm_g, 'rw_mu': rw_mu, 'rw_w_rkv': rw_w_rkv, 'rw_w0': rw_w0, 'rw_w1': rw_w1,
        'rw_w2': rw_w2, 'rw_a0': rw_a0, 'rw_a1': rw_a1, 'rw_a2': rw_a2, 'rw_v0': rw_v0, 'rw_v1': rw_v1,
        'rw_v2': rw_v2, 'rw_g1': rw_g1, 'rw_g2': rw_g2, 'rw_k_k': rw_k_k, 'rw_k_a': rw_k_a,
        'rw_r_k': rw_r_k, 'rw_ln_w': rw_ln_w, 'rw_ln_b': rw_ln_b, 'rw_w_o': rw_w_o,
        'ml_w_in': ml_w_in, 'ml_b_gates': ml_b_gates, 'ml_hn_w': ml_hn_w, 'ml_w_out': ml_w_out,
        'ffn_w_gu': ffn_w_gu, 'ffn_w_down': ffn_w_down,
    }
    f32 = jnp.float32
    B = x_prompt.shape[0]
    zero_shift = jnp.zeros((N_RWKV, B, D_MODEL), f32)
    zero_S = jnp.zeros((N_RWKV, B, RW_HEADS, RW_HEAD, RW_HEAD), f32)
    zero_C = jnp.zeros((N_MLSTM, B, ML_HEADS, ML_DQK, ML_DV), f32)
    zero_n = jnp.zeros((N_MLSTM, B, ML_HEADS, ML_DQK), f32)
    zero_m = jnp.zeros((N_MLSTM, B, ML_HEADS), f32)
    y_prompt, sh_p, S_p, C_p, n_p, m_p = _trunk(x_prompt, zero_shift, zero_S, zero_C, zero_n, zero_m, p)
    y_sample, sh_s, S_s, C_s, n_s, m_s = _trunk(x_sample, state_rwkv_shift, state_rwkv_S, state_mlstm_C,
                                                  state_mlstm_n, state_mlstm_m, p)
    return (y_prompt, y_sample, sh_p, S_p, C_p, n_p, m_p, sh_s, S_s, C_s, n_s, m_s)
```

```python
import functools
import math

import jax
import jax.numpy as jnp
from jax import lax
from jax.experimental import pallas as pl
from jax.experimental.pallas import tpu as pltpu

F32 = jnp.float32
BF16 = jnp.bfloat16

DEPTH = 4
N_MIXERS = 2
CHUNK = 64
RW_HEAD = 64
ML_HEADS = 4
GN_EPS = 64e-5
GATE_CAP = 15.0
EPS = 1e-6

LANES = 128
VMEM_LIMIT = 56 * 1024 * 1024
TOKEN_TILE = 256
GATE_ROWS = 16


def _const_spec(shape):
    nd = len(shape)
    return pl.BlockSpec(shape, lambda *_: (0,) * nd, pipeline_mode=pl.Buffered(1))


def _params(sem, vmem=VMEM_LIMIT):
    return pltpu.CompilerParams(dimension_semantics=sem, vmem_limit_bytes=vmem)


def _rms(x, g):
    return x * lax.rsqrt(jnp.mean(x * x, axis=-1, keepdims=True) + EPS) * g


def _mm(a, w):
    return jnp.dot(a.astype(BF16), w, preferred_element_type=F32)


def _split2(x):
    hi = x.astype(BF16)
    lo = (x - hi.astype(F32)).astype(BF16)
    return hi, lo


def _split3(x):
    hi = x.astype(BF16)
    r1 = x - hi.astype(F32)
    mid = r1.astype(BF16)
    lo = (r1 - mid.astype(F32)).astype(BF16)
    return hi, mid, lo


def _dot_exact_rhs(x, w01):
    hi, lo = _split2(x)
    return (jnp.dot(hi, w01, preferred_element_type=F32)
            + jnp.dot(lo, w01, preferred_element_type=F32))


def _dot_exact_lhs(w01, x):
    hi, mid, lo = _split3(x)
    return (jnp.dot(w01, hi, preferred_element_type=F32)
            + jnp.dot(w01, mid, preferred_element_type=F32)
            + jnp.dot(w01, lo, preferred_element_type=F32))


def _rwkv_pre_kernel(has_vfirst, nct, *refs):
    (x_ref, shift0_ref, g0_ref, mu_ref, wrkv_ref, w0_ref, w1_ref, w2_ref, a0_ref, a1_ref, a2_ref,
     g1_ref, g2_ref, kk_ref, ka_ref, hsum_ref, hexp_ref, tri_ref, sel_ref) = refs[:19]
    refs = refs[19:]
    if has_vfirst:
        v0_ref, v1_ref, v2_ref, vfirst_ref = refs[:4]
        refs = refs[4:]
    rt_ref, at_ref, bt_ref, kt_ref, v_ref, g_ref, pt_ref, shift_ref, carry = refs

    tm = x_ref.shape[1]

    @pl.when(pl.program_id(1) == 0)
    def _():
        carry[...] = shift0_ref[0]

    h = _rms(x_ref[0], g0_ref[...])
    row = lax.broadcasted_iota(jnp.int32, h.shape, 0)
    prev = jnp.where(row == 0, carry[...], pltpu.roll(h, 1, axis=0))
    last = h[tm - 1:tm, :]
    carry[...] = last
    shift_ref[0] = last

    dx = prev - h
    xr, xw, xk, xv, xa, xg = (h + dx * mu_ref[c:c + 1, :] for c in range(6))
    r = _mm(xr, wrkv_ref[0])
    k = _mm(xk, wrkv_ref[1])
    v = _mm(xv, wrkv_ref[2])
    wl = w0_ref[...] + _mm(jnp.tanh(_mm(xw, w1_ref[...])), w2_ref[...])
    lw = (-math.exp(-0.5)) * jax.nn.sigmoid(wl)
    a = jax.nn.sigmoid(a0_ref[...] + _mm(_mm(xa, a1_ref[...]), a2_ref[...]))
    g = _mm(jax.nn.sigmoid(_mm(xg, g1_ref[...])), g2_ref[...])
    if has_vfirst:
        mix = jax.nn.sigmoid(v0_ref[...] + _mm(_mm(xv, v1_ref[...]), v2_ref[...]))
        v = v + (vfirst_ref[0] - v) * mix

    kkr = k * kk_ref[...]
    ssq = _dot_exact_rhs(kkr * kkr, hsum_ref[...])
    inv = lax.rsqrt(jnp.maximum(ssq, 1e-24))
    kk = kkr * _dot_exact_rhs(inv, hexp_ref[...])
    kmod = k * (1.0 + (a - 1.0) * ka_ref[...])

    b = _dot_exact_lhs(tri_ref[...], lw)
    eb = jnp.exp(b)
    enb = jnp.exp(-b)
    rt_ref[0] = r * eb
    at_ref[0] = -kk * jnp.exp(b - lw)
    bt_ref[0] = kk * a * enb
    kt_ref[0] = kmod * enb
    v_ref[0] = v
    g_ref[0] = g
    pt_ref[0, 0] = jnp.exp(_dot_exact_lhs(sel_ref[...], lw))[:nct]


def _rwkv_pre(x, shift0, p, j, v_first):
    B, T, D = x.shape
    L = min(CHUNK, T)
    tm = min(TOKEN_TILE, T)
    nct = tm // L
    has_vfirst = v_first is not None
    row = jnp.arange(tm)
    tri = ((row[:, None] >= row[None, :]) & (row[:, None] // L == row[None, :] // L)).astype(BF16)
    sel = (jnp.arange(GATE_ROWS)[:, None] == row[None, :] // L).astype(BF16)

    tile = pl.BlockSpec((1, tm, D), lambda b, t: (b, t, 0))
    consts = [p['g0'], p['mu'], p['w_rkv'], p['w0'], p['w1'], p['w2'], p['a0'], p['a1'], p['a2'],
              p['g1'], p['g2'], p['k_k'], p['k_a'], p['hsum'], p['hexp'], tri, sel]
    args = [x, shift0.reshape(B, 1, D)] + consts
    specs = [tile, pl.BlockSpec((1, 1, D), lambda b, t: (b, 0, 0))] + [_const_spec(c.shape) for c in consts]
    if has_vfirst:
        vc = [p['v0'], p['v1'], p['v2']]
        args += vc + [v_first]
        specs += [_const_spec(c.shape) for c in vc] + [tile]

    big = jax.ShapeDtypeStruct((B, T, D), F32)
    outs = pl.pallas_call(
        functools.partial(_rwkv_pre_kernel, has_vfirst, nct),
        grid=(B, T // tm),
        in_specs=specs,
        out_specs=[tile] * 6 + [pl.BlockSpec((1, 1, nct, D), lambda b, t: (b, t, 0, 0)),
                                pl.BlockSpec((1, 1, D), lambda b, t: (b, 0, 0))],
        out_shape=[big] * 6 + [jax.ShapeDtypeStruct((B, T // tm, nct, D), F32),
                               jax.ShapeDtypeStruct((B, 1, D), F32)],
        scratch_shapes=[pltpu.VMEM((1, D), F32)],
        compiler_params=_params(("parallel", "arbitrary")),
        name="rwkv_pre",
    )(*args)
    rt, at, bt, kt, v, g, pt, shift = outs
    return rt, at, bt, kt, v, g, pt.reshape(B, T // L, 1, D), shift.reshape(B, D)


def _rwkv_scan_kernel(rt_ref, at_ref, bt_ref, kt_ref, v_ref, pt_ref, s0_ref, lnw_ref, lnb_ref, rk_ref,
                      o_ref, sT_ref, state):
    L = rt_ref.shape[1]
    N = RW_HEAD

    @pl.when(pl.program_id(2) == 0)
    def _():
        state[...] = s0_ref[0]

    row = lax.broadcasted_iota(jnp.int32, (L, L), 0)
    col = lax.broadcasted_iota(jnp.int32, (L, L), 1)
    strict = row > col
    incl = row >= col
    nt = (((1,), (1,)), ((), ()))
    tn = (((0,), (0,)), ((), ()))

    outs = []
    for i in range(LANES // N):
        sl = slice(i * N, (i + 1) * N)
        at = at_ref[0, :, sl]
        rt = rt_ref[0, :, sl]
        bt = bt_ref[0, :, sl]
        kt = kt_ref[0, :, sl]
        vh = v_ref[0, :, sl]
        pt = pt_ref[0, 0, :, sl]
        s = state[i]
        vb = vh.astype(BF16)

        x = jnp.concatenate([at, rt], axis=0).astype(BF16)
        gb = lax.dot_general(x, bt.astype(BF16), nt, preferred_element_type=F32)
        gk = lax.dot_general(x, kt.astype(BF16), nt, preferred_element_type=F32)
        xs = lax.dot_general(x, s.astype(BF16), nt, preferred_element_type=F32)
        n_pow = jnp.where(strict, gb[:L], 0.0)
        a_ak = jnp.where(strict, gk[:L], 0.0)
        a_rb = jnp.where(incl, gb[L:], 0.0)
        a_rk = jnp.where(incl, gk[L:], 0.0)

        u = xs[:L] + _mm(a_ak, vb)
        span = 1
        while True:
            u = u + _mm(n_pow, u.astype(BF16))
            span *= 2
            if span >= L:
                break
            n_pow = _mm(n_pow, n_pow.astype(BF16))
        ub = u.astype(BF16)

        o = xs[L:] + _mm(a_rb, ub) + _mm(a_rk, vb)
        state[i] = (s * pt
                    + lax.dot_general(ub, (bt * pt).astype(BF16), tn, preferred_element_type=F32)
                    + lax.dot_general(vb, (kt * pt).astype(BF16), tn, preferred_element_type=F32))

        mean = jnp.mean(o, axis=-1, keepdims=True)
        d = o - mean
        var = jnp.mean(d * d, axis=-1, keepdims=True)
        o = d * lax.rsqrt(var + GN_EPS) * lnw_ref[:, sl] + lnb_ref[:, sl]
        bonus = jnp.sum(rt * kt * rk_ref[:, sl], axis=-1, keepdims=True) * vh
        outs.append(o + bonus)

    o_ref[0] = jnp.concatenate(outs, axis=1)

    @pl.when(pl.program_id(2) == pl.num_programs(2) - 1)
    def _():
        sT_ref[0] = state[...]


def _rwkv_scan(rt, at, bt, kt, v, pt, s0, p):
    B, T, D = rt.shape
    L = min(CHUNK, T)
    hpb = LANES // RW_HEAD
    H = D // RW_HEAD
    tile = pl.BlockSpec((1, L, LANES), lambda b, h, c: (b, c, h))
    vec = pl.BlockSpec((1, LANES), lambda b, h, c: (0, h))
    st = pl.BlockSpec((1, hpb, RW_HEAD, RW_HEAD), lambda b, h, c: (b, h, 0, 0))
    return pl.pallas_call(
        _rwkv_scan_kernel,
        grid=(B, H // hpb, T // L),
        in_specs=[tile] * 5 + [pl.BlockSpec((1, 1, 1, LANES), lambda b, h, c: (b, c, 0, h)), st, vec, vec, vec],
        out_specs=[tile, st],
        out_shape=[jax.ShapeDtypeStruct((B, T, D), F32), jax.ShapeDtypeStruct(s0.shape, F32)],
        scratch_shapes=[pltpu.VMEM((hpb, RW_HEAD, RW_HEAD), F32)],
        compiler_params=_params(("parallel", "parallel", "arbitrary")),
        name="rwkv_scan",
    )(rt, at, bt, kt, v, pt, s0, p['ln_w'], p['ln_b'], p['r_k'])


def _gated_out_kernel(x_ref, a_ref, b_ref, w_ref, g_ref, o_ref):
    y = _mm(a_ref[...] * b_ref[...], w_ref[...])
    o_ref[...] = x_ref[...] + _rms(y, g_ref[...])


def _gated_out(x, a, b, w, g):
    R, D = x.shape
    tm = min(TOKEN_TILE, R)
    tile = pl.BlockSpec((tm, D), lambda t: (t, 0))
    return pl.pallas_call(
        _gated_out_kernel,
        grid=(R // tm,),
        in_specs=[tile, tile, tile, _const_spec(w.shape), _const_spec(g.shape)],
        out_specs=tile,
        out_shape=jax.ShapeDtypeStruct((R, D), F32),
        compiler_params=_params(("parallel",)),
        name="gated_out",
    )(x, a, b, w, g)


def _ffn_kernel(x_ref, gpre_ref, wgu_ref, wd_ref, gpost_ref, o_ref):
    x = x_ref[...]
    f = wd_ref.shape[0]
    gu = _mm(_rms(x, gpre_ref[...]), wgu_ref[...])
    gate = gu[:, :f]
    act = gate * jax.nn.sigmoid(gate) * gu[:, f:]
    o_ref[...] = x + _rms(_mm(act, wd_ref[...]), gpost_ref[...])


def _ffn(x, gpre, wgu, wd, gpost):
    R, D = x.shape
    tm = min(TOKEN_TILE, R)
    tile = pl.BlockSpec((tm, D), lambda t: (t, 0))
    return pl.pallas_call(
        _ffn_kernel,
        grid=(R // tm,),
        in_specs=[tile] + [_const_spec(c.shape) for c in (gpre, wgu, wd, gpost)],
        out_specs=tile,
        out_shape=jax.ShapeDtypeStruct((R, D), F32),
        compiler_params=_params(("parallel",)),
        name="ffn",
    )(x, gpre, wgu, wd, gpost)


def _mlstm_pre_kernel(L, nq, x_ref, g0_ref, win_ref, wgc_ref, wgr_ref, bc_ref, br_ref,
                      q_ref, k_ref, v_ref, og_ref, gc_ref, gr_ref):
    h = _rms(x_ref[...], g0_ref[...]).astype(BF16)
    z = jnp.dot(h, win_ref[...], preferred_element_type=F32)
    nv = v_ref.shape[1]
    dqk = nq // ML_HEADS
    q_ref[...] = z[:, :nq] * (dqk ** -0.5)
    k_ref[...] = z[:, nq:2 * nq]
    v_ref[...] = z[:, 2 * nq:2 * nq + nv]
    og_ref[...] = jax.nn.sigmoid(z[:, 2 * nq + nv:])

    def cap(t):
        return GATE_CAP * jnp.tanh(t / GATE_CAP)

    gc = cap(jnp.dot(h, wgc_ref[...], preferred_element_type=F32) + bc_ref[...])
    lane = lax.broadcasted_iota(jnp.int32, gc.shape, 1)
    gc_ref[...] = jnp.where(lane < ML_HEADS, gc, jax.nn.log_sigmoid(gc))
    gr = cap(lax.dot_general(wgr_ref[...], h, (((1,), (1,)), ((), ())), preferred_element_type=F32)
             + br_ref[...])
    rowi = lax.broadcasted_iota(jnp.int32, gr.shape, 0)
    gr = jnp.where(rowi < ML_HEADS, gr, jax.nn.log_sigmoid(gr))
    for c in range(gr_ref.shape[0]):
        gr_ref[c] = gr[:, c * L:(c + 1) * L]


def _mlstm_pre(x, L, p):
    R, D = x.shape
    tm = min(TOKEN_TILE, R)
    nq = ML_HEADS * (D // (2 * ML_HEADS))
    nv = D
    consts = [p['g0'], p['w_in'], p['w_gc'], p['w_gr'], p['b_c'], p['b_r']]
    row = lambda w: pl.BlockSpec((tm, w), lambda t: (t, 0))
    return pl.pallas_call(
        functools.partial(_mlstm_pre_kernel, L, nq),
        grid=(R // tm,),
        in_specs=[row(D)] + [_const_spec(c.shape) for c in consts],
        out_specs=[row(nq), row(nq), row(nv), row(nv), row(LANES),
                   pl.BlockSpec((tm // L, GATE_ROWS, L), lambda t: (t, 0, 0))],
        out_shape=[jax.ShapeDtypeStruct((R, nq), F32), jax.ShapeDtypeStruct((R, nq), F32),
                   jax.ShapeDtypeStruct((R, nv), F32), jax.ShapeDtypeStruct((R, nv), F32),
                   jax.ShapeDtypeStruct((R, LANES), F32),
                   jax.ShapeDtypeStruct((R // L, GATE_ROWS, L), F32)],
        compiler_params=_params(("parallel",)),
        name="mlstm_pre",
    )(x, *consts)


def _mlstm_scan_kernel(q_ref, k_ref, v_ref, gc_ref, gr_ref, c0_ref, n0_ref, m0_ref, hnw_ref,
                       h_ref, cT_ref, nT_ref, mT_ref, c_st, n_st, m_st):
    L = q_ref.shape[0]
    H = ML_HEADS
    dqk = q_ref.shape[1] // H
    dv = v_ref.shape[1] // H

    @pl.when(pl.program_id(1) == 0)
    def _():
        c_st[...] = c0_ref[0]
        n_st[...] = n0_ref[0]
        m_st[...] = m0_ref[0]

    row = lax.broadcasted_iota(jnp.int32, (L, L), 0)
    col = lax.broadcasted_iota(jnp.int32, (L, L), 1)
    causal = row >= col
    tril = causal.astype(BF16)
    triu = (row <= col).astype(BF16)
    gc = gc_ref[...]
    gr = gr_ref[0]
    bc = _dot_exact_lhs(tril, gc)
    hi, mid, lo = _split3(gr)
    br = (jnp.dot(hi, triu, preferred_element_type=F32) + jnp.dot(mid, triu, preferred_element_type=F32)
          + jnp.dot(lo, triu, preferred_element_type=F32))

    nt = (((1,), (1,)), ((), ()))
    tn = (((0,), (0,)), ((), ()))
    outs = []
    for hd in range(H):
        qh = q_ref[:, hd * dqk:(hd + 1) * dqk]
        kh = k_ref[:, hd * dqk:(hd + 1) * dqk]
        vb = v_ref[:, hd * dv:(hd + 1) * dv].astype(BF16)
        qb = qh.astype(BF16)
        li_c = gc[:, hd:hd + 1]
        b_c = bc[:, H + hd:H + hd + 1]
        li_r = gr[hd:hd + 1, :]
        b_r = br[H + hd:H + hd + 1, :]
        m_prev = m_st[hd]
        c_prev = c_st[hd]
        n_prev = n_st[hd]

        dmat = jnp.where(causal, b_c - b_r + li_r, -jnp.inf)
        inter = b_c + m_prev
        m_t = jnp.maximum(inter, jnp.max(dmat, axis=-1, keepdims=True))
        wts = jnp.exp(dmat - m_t)
        sc = jnp.exp(inter - m_t)
        qk = lax.dot_general(qb, kh.astype(BF16), nt, preferred_element_type=F32) * wts
        num = _mm(qk, vb) + sc * _mm(qb, c_prev.astype(BF16))
        den = jnp.sum(qk, axis=-1, keepdims=True) + sc * jnp.sum(qh * n_prev, axis=-1, keepdims=True)
        h = num / jnp.maximum(jnp.abs(den), jnp.exp(-m_t))

        m_new = m_t[L - 1:L, :]
        b_last = b_c[L - 1:L, :]
        kw = kh * jnp.exp(b_last - b_c + li_c - m_new)
        sc_end = jnp.exp(b_last + m_prev - m_new)
        c_st[hd] = sc_end * c_prev + lax.dot_general(kw.astype(BF16), vb, tn, preferred_element_type=F32)
        n_st[hd] = sc_end * n_prev + jnp.sum(kw, axis=0, keepdims=True)
        m_st[hd] = m_new

        h = h * lax.rsqrt(jnp.mean(h * h, axis=-1, keepdims=True) + EPS)
        outs.append(h * hnw_ref[:, hd * dv:(hd + 1) * dv])

    h_ref[...] = jnp.concatenate(outs, axis=1)

    @pl.when(pl.program_id(1) == pl.num_programs(1) - 1)
    def _():
        cT_ref[0] = c_st[...]
        nT_ref[0] = n_st[...]
        mT_ref[0] = m_st[...]


def _mlstm_scan(q, k, v, gc, gr, c0, n0, m0, hn_w, B, T):
    L = min(CHUNK, T)
    nc = T // L
    H = ML_HEADS
    nq, nv = q.shape[1], v.shape[1]
    dqk, dv = nq // H, nv // H
    row = lambda w: pl.BlockSpec((L, w), lambda b, c: (b * nc + c, 0))
    st = lambda *s: pl.BlockSpec((1,) + s, lambda b, c: (b,) + (0,) * len(s))
    return pl.pallas_call(
        _mlstm_scan_kernel,
        grid=(B, nc),
        in_specs=[row(nq), row(nq), row(nv), row(LANES),
                  pl.BlockSpec((1, GATE_ROWS, L), lambda b, c: (b * nc + c, 0, 0)),
                  st(H, dqk, dv), st(H, 1, dqk), st(H, 1, 1), _const_spec(hn_w.shape)],
        out_specs=[row(nv), st(H, dqk, dv), st(H, 1, dqk), st(H, 1, 1)],
        out_shape=[jax.ShapeDtypeStruct((B * T, nv), F32), jax.ShapeDtypeStruct((B, H, dqk, dv), F32),
                   jax.ShapeDtypeStruct((B, H, 1, dqk), F32), jax.ShapeDtypeStruct((B, H, 1, 1), F32)],
        scratch_shapes=[pltpu.VMEM((H, dqk, dv), F32), pltpu.VMEM((H, 1, dqk), F32),
                        pltpu.VMEM((H, 1, 1), F32)],
        compiler_params=_params(("parallel", "arbitrary")),
        name="mlstm_scan",
    )(q, k, v, gc, gr, c0, n0.reshape(B, H, 1, dqk), m0.reshape(B, H, 1, 1), hn_w)


def _prep_weights(w):
    D = w['norm_g'].shape[-1]
    H = D // RW_HEAD
    row = lambda a: a.reshape(1, -1).astype(F32)
    head_of_lane = jnp.arange(D) // RW_HEAD
    hsum = (head_of_lane[:, None] == jnp.arange(LANES)[None, :]).astype(BF16)
    hexp = hsum.T
    layers = []
    for i in range(DEPTH):
        j = i // N_MIXERS
        ng = w['norm_g'][i]
        lp = {'g0': row(ng[0]), 'g_mix': row(ng[1]), 'g_ffn_pre': row(ng[2]), 'g_ffn_post': row(ng[3]),
              'ffn_w_gu': w['ffn_w_gu'][i].astype(BF16), 'ffn_w_down': w['ffn_w_down'][i].astype(BF16)}
        if i % N_MIXERS == 0:
            lp.update({
                'mu': jnp.pad(w['rw_mu'][j], ((0, 2), (0, 0))),
                'w_rkv': w['rw_w_rkv'][j].astype(BF16),
                'w0': row(w['rw_w0'][j]), 'w1': w['rw_w1'][j].astype(BF16), 'w2': w['rw_w2'][j].astype(BF16),
                'a0': row(w['rw_a0'][j]), 'a1': w['rw_a1'][j].astype(BF16), 'a2': w['rw_a2'][j].astype(BF16),
                'g1': w['rw_g1'][j].astype(BF16), 'g2': w['rw_g2'][j].astype(BF16),
                'k_k': row(w['rw_k_k'][j]), 'k_a': row(w['rw_k_a'][j]), 'r_k': row(w['rw_r_k'][j]),
                'ln_w': row(w['rw_ln_w'][j]), 'ln_b': row(w['rw_ln_b'][j]),
                'w_o': w['rw_w_o'][j].astype(BF16), 'hsum': hsum, 'hexp': hexp,
            })
            if j > 0:
                lp.update({'v0': row(w['rw_v0'][j - 1]), 'v1': w['rw_v1'][j - 1].astype(BF16),
                           'v2': w['rw_v2'][j - 1].astype(BF16)})
        else:
            w_in = w['ml_w_in'][j]
            n_main = w_in.shape[1] - 2 * ML_HEADS
            w_g = w_in[:, n_main:]
            bias = w['ml_b_gates'][j].astype(F32)
            lp.update({
                'w_in': w_in[:, :n_main].astype(BF16),
                'w_gc': jnp.pad(w_g, ((0, 0), (0, LANES - 2 * ML_HEADS))).astype(BF16),
                'w_gr': jnp.pad(w_g.T, ((0, GATE_ROWS - 2 * ML_HEADS), (0, 0))).astype(BF16),
                'b_c': jnp.pad(bias, (0, LANES - 2 * ML_HEADS)).reshape(1, LANES),
                'b_r': jnp.pad(bias, (0, GATE_ROWS - 2 * ML_HEADS)).reshape(GATE_ROWS, 1),
                'hn_w': row(w['ml_hn_w'][j]), 'w_out': w['ml_w_out'][j].astype(BF16),
            })
        layers.append(lp)
    return layers


def _trunk(x, rw_shift, rw_S, ml_C, ml_n, ml_m, layers):
    B, T, D = x.shape
    L = min(CHUNK, T)
    flat = lambda a: a.reshape(B * T, a.shape[-1])
    v_first = None
    shifts, Ss, Cs, ns, ms = [], [], [], [], []
    xf = flat(x)
    for i, lp in enumerate(layers):
        j = i // N_MIXERS
        if i % N_MIXERS == 0:
            rt, at, bt, kt, v, g, pt, sh = _rwkv_pre(xf.reshape(B, T, D), rw_shift[j], lp, j, v_first)
            if j == 0:
                v_first = v
            o, S = _rwkv_scan(rt, at, bt, kt, v, pt, rw_S[j], lp)
            shifts.append(sh)
            Ss.append(S)
            xf = _gated_out(xf, flat(o), flat(g), lp['w_o'], lp['g_mix'])
        else:
            q, k, v, og, gc, gr = _mlstm_pre(xf, L, lp)
            hn, C, n, m = _mlstm_scan(q, k, v, gc, gr, ml_C[j], ml_n[j], ml_m[j], lp['hn_w'], B, T)
            Cs.append(C)
            ns.append(n.reshape(B, ML_HEADS, -1))
            ms.append(m.reshape(B, ML_HEADS))
            xf = _gated_out(xf, og, hn, lp['w_out'], lp['g_mix'])
        xf = _ffn(xf, lp['g_ffn_pre'], lp['ffn_w_gu'], lp['ffn_w_down'], lp['g_ffn_post'])
    return xf.reshape(B, T, D), jnp.stack(shifts), jnp.stack(Ss), jnp.stack(Cs), jnp.stack(ns), jnp.stack(ms)


def kernel(x_prompt, x_sample, state_rwkv_shift, state_rwkv_S, state_mlstm_C, state_mlstm_n, state_mlstm_m,
           norm_g, rw_mu, rw_w_rkv, rw_w0, rw_w1, rw_w2, rw_a0, rw_a1, rw_a2, rw_v0, rw_v1, rw_v2,
           rw_g1, rw_g2, rw_k_k, rw_k_a, rw_r_k, rw_ln_w, rw_ln_b, rw_w_o,
           ml_w_in, ml_b_gates, ml_hn_w, ml_w_out, ffn_w_gu, ffn_w_down):
    w = {
        'norm_g': norm_g, 'rw_mu': rw_mu, 'rw_w_rkv': rw_w_rkv, 'rw_w0': rw_w0, 'rw_w1': rw_w1,
        'rw_w2': rw_w2, 'rw_a0': rw_a0, 'rw_a1': rw_a1, 'rw_a2': rw_a2, 'rw_v0': rw_v0, 'rw_v1': rw_v1,
        'rw_v2': rw_v2, 'rw_g1': rw_g1, 'rw_g2': rw_g2, 'rw_k_k': rw_k_k, 'rw_k_a': rw_k_a,
        'rw_r_k': rw_r_k, 'rw_ln_w': rw_ln_w, 'rw_ln_b': rw_ln_b, 'rw_w_o': rw_w_o,
        'ml_w_in': ml_w_in, 'ml_b_gates': ml_b_gates, 'ml_hn_w': ml_hn_w, 'ml_w_out': ml_w_out,
        'ffn_w_gu': ffn_w_gu, 'ffn_w_down': ffn_w_down,
    }
    layers = _prep_weights(w)
    Bp = x_prompt.shape[0]
    zeros = lambda a: jnp.zeros((a.shape[0], Bp) + a.shape[2:], F32)
    out_p = _trunk(x_prompt, zeros(state_rwkv_shift), zeros(state_rwkv_S), zeros(state_mlstm_C),
                   zeros(state_mlstm_n), zeros(state_mlstm_m), layers)
    out_s = _trunk(x_sample, state_rwkv_shift, state_rwkv_S, state_mlstm_C, state_mlstm_n, state_mlstm_m,
                   layers)
    return (out_p[0], out_s[0]) + tuple(out_p[1:]) + tuple(out_s[1:])
```

```python
import functools
import math

import jax
import jax.numpy as jnp
from jax import lax
from jax.experimental import pallas as pl
from jax.experimental.pallas import tpu as pltpu

F32 = jnp.float32
BF16 = jnp.bfloat16

DEPTH = 4
N_MIXERS = 2
CHUNK = 64
RW_HEAD = 64
ML_HEADS = 4
GN_EPS = 64e-5
GATE_CAP = 15.0
EPS = 1e-6

LANES = 128
VMEM_LIMIT = 56 * 1024 * 1024
TOKEN_TILE = 256
SCAN_TILE = 256
GATE_ROWS = 16


def _const_spec(shape):
    nd = len(shape)
    return pl.BlockSpec(shape, lambda *_: (0,) * nd, pipeline_mode=pl.Buffered(1))


def _params(sem, vmem=VMEM_LIMIT):
    return pltpu.CompilerParams(dimension_semantics=sem, vmem_limit_bytes=vmem)


def _rms(x, g):
    return x * lax.rsqrt(jnp.mean(x * x, axis=-1, keepdims=True) + EPS) * g


def _mm(a, w):
    return jnp.dot(a.astype(BF16), w, preferred_element_type=F32)


def _split2(x):
    hi = x.astype(BF16)
    lo = (x - hi.astype(F32)).astype(BF16)
    return hi, lo


def _split3(x):
    hi = x.astype(BF16)
    r1 = x - hi.astype(F32)
    mid = r1.astype(BF16)
    lo = (r1 - mid.astype(F32)).astype(BF16)
    return hi, mid, lo


def _dot_exact_rhs(x, w01):
    hi, lo = _split2(x)
    return (jnp.dot(hi, w01, preferred_element_type=F32)
            + jnp.dot(lo, w01, preferred_element_type=F32))


def _dot_exact_lhs(w01, x):
    hi, mid, lo = _split3(x)
    return (jnp.dot(w01, hi, preferred_element_type=F32)
            + jnp.dot(w01, mid, preferred_element_type=F32)
            + jnp.dot(w01, lo, preferred_element_type=F32))


def _rwkv_pre_kernel(has_vfirst, nct, *refs):
    (x_ref, shift0_ref, g0_ref, mu_ref, wrkv_ref, w0_ref, w1_ref, w2_ref, a0_ref, a1_ref, a2_ref,
     g1_ref, g2_ref, kk_ref, ka_ref, hsum_ref, hexp_ref, tri_ref, sel_ref) = refs[:19]
    refs = refs[19:]
    if has_vfirst:
        v0_ref, v1_ref, v2_ref, vfirst_ref = refs[:4]
        refs = refs[4:]
    rt_ref, at_ref, bt_ref, kt_ref, v_ref, g_ref, pt_ref, shift_ref, carry = refs

    tm = x_ref.shape[1]

    @pl.when(pl.program_id(1) == 0)
    def _():
        carry[...] = shift0_ref[0]

    h = _rms(x_ref[0], g0_ref[...])
    row = lax.broadcasted_iota(jnp.int32, h.shape, 0)
    prev = jnp.where(row == 0, carry[...], pltpu.roll(h, 1, axis=0))
    last = h[tm - 1:tm, :]
    carry[...] = last
    shift_ref[0] = last

    dx = prev - h
    xr, xw, xk, xv, xa, xg = (h + dx * mu_ref[c:c + 1, :] for c in range(6))
    r = _mm(xr, wrkv_ref[0])
    k = _mm(xk, wrkv_ref[1])
    v = _mm(xv, wrkv_ref[2])
    wl = w0_ref[...] + _mm(jnp.tanh(_mm(xw, w1_ref[...])), w2_ref[...])
    lw = (-math.exp(-0.5)) * jax.nn.sigmoid(wl)
    a = jax.nn.sigmoid(a0_ref[...] + _mm(_mm(xa, a1_ref[...]), a2_ref[...]))
    g = _mm(jax.nn.sigmoid(_mm(xg, g1_ref[...])), g2_ref[...])
    if has_vfirst:
        mix = jax.nn.sigmoid(v0_ref[...] + _mm(_mm(xv, v1_ref[...]), v2_ref[...]))
        v = v + (vfirst_ref[0] - v) * mix

    kkr = k * kk_ref[...]
    ssq = _dot_exact_rhs(kkr * kkr, hsum_ref[...])
    inv = lax.rsqrt(jnp.maximum(ssq, 1e-24))
    kk = kkr * _dot_exact_rhs(inv, hexp_ref[...])
    kmod = k * (1.0 + (a - 1.0) * ka_ref[...])

    b = _dot_exact_lhs(tri_ref[...], lw)
    eb = jnp.exp(b)
    enb = jnp.exp(-b)
    rt_ref[0] = r * eb
    at_ref[0] = -kk * jnp.exp(b - lw)
    bt_ref[0] = kk * a * enb
    kt_ref[0] = kmod * enb
    v_ref[0] = v
    g_ref[0] = g
    pt_ref[0, 0] = jnp.exp(_dot_exact_lhs(sel_ref[...], lw))[:nct]


def _rwkv_pre(x, shift0, p, j, v_first):
    B, T, D = x.shape
    L = min(CHUNK, T)
    tm = min(TOKEN_TILE, T)
    nct = tm // L
    has_vfirst = v_first is not None
    row = jnp.arange(tm)
    tri = ((row[:, None] >= row[None, :]) & (row[:, None] // L == row[None, :] // L)).astype(BF16)
    sel = (jnp.arange(GATE_ROWS)[:, None] == row[None, :] // L).astype(BF16)

    tile = pl.BlockSpec((1, tm, D), lambda b, t: (b, t, 0))
    consts = [p['g0'], p['mu'], p['w_rkv'], p['w0'], p['w1'], p['w2'], p['a0'], p['a1'], p['a2'],
              p['g1'], p['g2'], p['k_k'], p['k_a'], p['hsum'], p['hexp'], tri, sel]
    args = [x, shift0.reshape(B, 1, D)] + consts
    specs = [tile, pl.BlockSpec((1, 1, D), lambda b, t: (b, 0, 0))] + [_const_spec(c.shape) for c in consts]
    if has_vfirst:
        vc = [p['v0'], p['v1'], p['v2']]
        args += vc + [v_first]
        specs += [_const_spec(c.shape) for c in vc] + [tile]

    big = jax.ShapeDtypeStruct((B, T, D), F32)
    outs = pl.pallas_call(
        functools.partial(_rwkv_pre_kernel, has_vfirst, nct),
        grid=(B, T // tm),
        in_specs=specs,
        out_specs=[tile] * 6 + [pl.BlockSpec((1, 1, nct, D), lambda b, t: (b, t, 0, 0)),
                                pl.BlockSpec((1, 1, D), lambda b, t: (b, 0, 0))],
        out_shape=[big] * 6 + [jax.ShapeDtypeStruct((B, T // tm, nct, D), F32),
                               jax.ShapeDtypeStruct((B, 1, D), F32)],
        scratch_shapes=[pltpu.VMEM((1, D), F32)],
        compiler_params=_params(("parallel", "arbitrary")),
        name="rwkv_pre",
    )(*args)
    rt, at, bt, kt, v, g, pt, shift = outs
    return rt, at, bt, kt, v, g, pt.reshape(B, T // L, 1, D), shift.reshape(B, D)


def _rwkv_chunk_maps(L, items):
    N = RW_HEAD
    nt = (((1,), (1,)), ((), ()))
    tn = (((0,), (0,)), ((), ()))
    row = lax.broadcasted_iota(jnp.int32, (L, 2 * L), 0)
    col = lax.broadcasted_iota(jnp.int32, (L, 2 * L), 1)
    col = jnp.where(col >= L, col - L, col)
    strict = row > col
    incl = row >= col
    zeros = jnp.zeros((L, N), F32)

    gs = [lax.dot_general(jnp.concatenate([at, rt], axis=0).astype(BF16),
                          jnp.concatenate([bt, kt], axis=0).astype(BF16), nt, preferred_element_type=F32)
          for at, rt, bt, kt, _, _ in items]
    tops = [jnp.where(strict, g[:L], 0.0) for g in gs]
    bots = [jnp.where(incl, g[L:], 0.0) for g in gs]
    wus = [jnp.concatenate([it[0], _mm(top, jnp.concatenate([zeros, it[4]], axis=0).astype(BF16))], axis=1)
           for it, top in zip(items, tops)]
    n_pows = [top[:, :L] for top in tops]
    span = 1
    while True:
        span *= 2
        if span >= L:
            wus = [wu + _mm(n_pow, wu.astype(BF16)) for wu, n_pow in zip(wus, n_pows)]
            break
        res = [_mm(n_pow, jnp.concatenate([wu, n_pow], axis=1).astype(BF16))
               for wu, n_pow in zip(wus, n_pows)]
        wus = [wu + r[:, :2 * N] for wu, r in zip(wus, res)]
        n_pows = [r[:, 2 * N:] for r in res]

    zs = [jnp.concatenate([wu, jnp.concatenate([zeros, it[4]], axis=1)], axis=0).astype(BF16)
          for it, wu in zip(items, wus)]
    ros = [_mm(bot, z) for bot, z in zip(bots, zs)]
    mcs = [lax.dot_general(z, jnp.concatenate([it[2] * it[5], it[3] * it[5]], axis=0).astype(BF16), tn,
                           preferred_element_type=F32)
           for it, z in zip(items, zs)]
    return [(it[1] + ro[:, :N], ro[:, N:], mc[:N], mc[N:]) for it, ro, mc in zip(items, ros, mcs)]


def _rwkv_scan_kernel(L, rt_ref, at_ref, bt_ref, kt_ref, v_ref, pt_ref, s0_ref, lnw_ref, lnb_ref, rk_ref,
                      o_ref, sT_ref, state):
    N = RW_HEAD
    nck = rt_ref.shape[1] // L
    heads = LANES // N

    @pl.when(pl.program_id(2) == 0)
    def _():
        state[...] = s0_ref[0]

    nt = (((1,), (1,)), ((), ()))
    where = [(slice(c * L, (c + 1) * L), slice(i * N, (i + 1) * N), c, i)
             for c in range(nck) for i in range(heads)]
    items = [(at_ref[0, rows, sl], rt_ref[0, rows, sl], bt_ref[0, rows, sl], kt_ref[0, rows, sl],
              v_ref[0, rows, sl], pt_ref[0, c, :, sl]) for rows, sl, c, i in where]
    maps = _rwkv_chunk_maps(L, items)

    s = [state[i] for i in range(heads)]
    for (rows, sl, c, i), it, (rhat, ohat, m, cc) in zip(where, items, maps):
        _, rt, _, kt, vh, pt = it
        sb = s[i].astype(BF16)
        o = lax.dot_general(rhat.astype(BF16), sb, nt, preferred_element_type=F32) + ohat
        s[i] = s[i] * pt + _mm(sb, m.astype(BF16)) + cc

        mean = jnp.mean(o, axis=-1, keepdims=True)
        d = o - mean
        var = jnp.mean(d * d, axis=-1, keepdims=True)
        o = d * lax.rsqrt(var + GN_EPS) * lnw_ref[:, sl] + lnb_ref[:, sl]
        bonus = jnp.sum(rt * kt * rk_ref[:, sl], axis=-1, keepdims=True) * vh
        o_ref[0, rows, sl] = o + bonus
    for i in range(heads):
        state[i] = s[i]

    @pl.when(pl.program_id(2) == pl.num_programs(2) - 1)
    def _():
        sT_ref[0] = state[...]


def _rwkv_scan(rt, at, bt, kt, v, pt, s0, p):
    B, T, D = rt.shape
    L = min(CHUNK, T)
    ts = min(SCAN_TILE, T)
    hpb = LANES // RW_HEAD
    H = D // RW_HEAD
    tile = pl.BlockSpec((1, ts, LANES), lambda b, h, c: (b, c, h))
    vec = pl.BlockSpec((1, LANES), lambda b, h, c: (0, h))
    st = pl.BlockSpec((1, hpb, RW_HEAD, RW_HEAD), lambda b, h, c: (b, h, 0, 0))
    return pl.pallas_call(
        functools.partial(_rwkv_scan_kernel, L),
        grid=(B, H // hpb, T // ts),
        in_specs=[tile] * 5 + [pl.BlockSpec((1, ts // L, 1, LANES), lambda b, h, c: (b, c, 0, h)), st, vec, vec, vec],
        out_specs=[tile, st],
        out_shape=[jax.ShapeDtypeStruct((B, T, D), F32), jax.ShapeDtypeStruct(s0.shape, F32)],
        scratch_shapes=[pltpu.VMEM((hpb, RW_HEAD, RW_HEAD), F32)],
        compiler_params=_params(("parallel", "parallel", "arbitrary")),
        name="rwkv_scan",
    )(rt, at, bt, kt, v, pt, s0, p['ln_w'], p['ln_b'], p['r_k'])


def _gated_out_kernel(x_ref, a_ref, b_ref, w_ref, g_ref, o_ref):
    y = _mm(a_ref[...] * b_ref[...], w_ref[...])
    o_ref[...] = x_ref[...] + _rms(y, g_ref[...])


def _gated_out(x, a, b, w, g):
    R, D = x.shape
    tm = min(TOKEN_TILE, R)
    tile = pl.BlockSpec((tm, D), lambda t: (t, 0))
    return pl.pallas_call(
        _gated_out_kernel,
        grid=(R // tm,),
        in_specs=[tile, tile, tile, _const_spec(w.shape), _const_spec(g.shape)],
        out_specs=tile,
        out_shape=jax.ShapeDtypeStruct((R, D), F32),
        compiler_params=_params(("parallel",)),
        name="gated_out",
    )(x, a, b, w, g)


def _ffn_kernel(x_ref, gpre_ref, wgu_ref, wd_ref, gpost_ref, o_ref):
    x = x_ref[...]
    f = wd_ref.shape[0]
    gu = _mm(_rms(x, gpre_ref[...]), wgu_ref[...])
    gate = gu[:, :f]
    act = gate * jax.nn.sigmoid(gate) * gu[:, f:]
    o_ref[...] = x + _rms(_mm(act, wd_ref[...]), gpost_ref[...])


def _ffn(x, gpre, wgu, wd, gpost):
    R, D = x.shape
    tm = min(TOKEN_TILE, R)
    tile = pl.BlockSpec((tm, D), lambda t: (t, 0))
    return pl.pallas_call(
        _ffn_kernel,
        grid=(R // tm,),
        in_specs=[tile] + [_const_spec(c.shape) for c in (gpre, wgu, wd, gpost)],
        out_specs=tile,
        out_shape=jax.ShapeDtypeStruct((R, D), F32),
        compiler_params=_params(("parallel",)),
        name="ffn",
    )(x, gpre, wgu, wd, gpost)


def _mlstm_pre_kernel(L, nq, x_ref, g0_ref, win_ref, wgc_ref, wgr_ref, bc_ref, br_ref,
                      q_ref, k_ref, v_ref, og_ref, gc_ref, gr_ref):
    h = _rms(x_ref[...], g0_ref[...]).astype(BF16)
    z = jnp.dot(h, win_ref[...], preferred_element_type=F32)
    nv = v_ref.shape[1]
    dqk = nq // ML_HEADS
    q_ref[...] = z[:, :nq] * (dqk ** -0.5)
    k_ref[...] = z[:, nq:2 * nq]
    v_ref[...] = z[:, 2 * nq:2 * nq + nv]
    og_ref[...] = jax.nn.sigmoid(z[:, 2 * nq + nv:])

    def cap(t):
        return GATE_CAP * jnp.tanh(t / GATE_CAP)

    gc = cap(jnp.dot(h, wgc_ref[...], preferred_element_type=F32) + bc_ref[...])
    lane = lax.broadcasted_iota(jnp.int32, gc.shape, 1)
    gc_ref[...] = jnp.where(lane < ML_HEADS, gc, jax.nn.log_sigmoid(gc))
    gr = cap(lax.dot_general(wgr_ref[...], h, (((1,), (1,)), ((), ())), preferred_element_type=F32)
             + br_ref[...])
    rowi = lax.broadcasted_iota(jnp.int32, gr.shape, 0)
    gr = jnp.where(rowi < ML_HEADS, gr, jax.nn.log_sigmoid(gr))
    for c in range(gr_ref.shape[0]):
        gr_ref[c] = gr[:, c * L:(c + 1) * L]


def _mlstm_pre(x, L, p):
    R, D = x.shape
    tm = min(TOKEN_TILE, R)
    nq = ML_HEADS * (D // (2 * ML_HEADS))
    nv = D
    consts = [p['g0'], p['w_in'], p['w_gc'], p['w_gr'], p['b_c'], p['b_r']]
    row = lambda w: pl.BlockSpec((tm, w), lambda t: (t, 0))
    return pl.pallas_call(
        functools.partial(_mlstm_pre_kernel, L, nq),
        grid=(R // tm,),
        in_specs=[row(D)] + [_const_spec(c.shape) for c in consts],
        out_specs=[row(nq), row(nq), row(nv), row(nv), row(LANES),
                   pl.BlockSpec((tm // L, GATE_ROWS, L), lambda t: (t, 0, 0))],
        out_shape=[jax.ShapeDtypeStruct((R, nq), F32), jax.ShapeDtypeStruct((R, nq), F32),
                   jax.ShapeDtypeStruct((R, nv), F32), jax.ShapeDtypeStruct((R, nv), F32),
                   jax.ShapeDtypeStruct((R, LANES), F32),
                   jax.ShapeDtypeStruct((R // L, GATE_ROWS, L), F32)],
        compiler_params=_params(("parallel",)),
        name="mlstm_pre",
    )(x, *consts)


def _mlstm_scan_kernel(q_ref, k_ref, v_ref, gc_ref, gr_ref, c0_ref, n0_ref, m0_ref, hnw_ref,
                       h_ref, cT_ref, nT_ref, mT_ref, c_st, n_st, m_st):
    L = q_ref.shape[0]
    H = ML_HEADS
    dqk = q_ref.shape[1] // H
    dv = v_ref.shape[1] // H

    @pl.when(pl.program_id(1) == 0)
    def _():
        c_st[...] = c0_ref[0]
        n_st[...] = n0_ref[0]
        m_st[...] = m0_ref[0]

    row = lax.broadcasted_iota(jnp.int32, (L, L), 0)
    col = lax.broadcasted_iota(jnp.int32, (L, L), 1)
    causal = row >= col
    tril = causal.astype(BF16)
    triu = (row <= col).astype(BF16)
    gc = gc_ref[...]
    gr = gr_ref[0]
    bc = _dot_exact_lhs(tril, gc)
    hi, mid, lo = _split3(gr)
    br = (jnp.dot(hi, triu, preferred_element_type=F32) + jnp.dot(mid, triu, preferred_element_type=F32)
          + jnp.dot(lo, triu, preferred_element_type=F32))

    nt = (((1,), (1,)), ((), ()))
    tn = (((0,), (0,)), ((), ()))
    outs = []
    for hd in range(H):
        qh = q_ref[:, hd * dqk:(hd + 1) * dqk]
        kh = k_ref[:, hd * dqk:(hd + 1) * dqk]
        vb = v_ref[:, hd * dv:(hd + 1) * dv].astype(BF16)
        qb = qh.astype(BF16)
        li_c = gc[:, hd:hd + 1]
        b_c = bc[:, H + hd:H + hd + 1]
        li_r = gr[hd:hd + 1, :]
        b_r = br[H + hd:H + hd + 1, :]
        m_prev = m_st[hd]
        c_prev = c_st[hd]
        n_prev = n_st[hd]

        dmat = jnp.where(causal, b_c - b_r + li_r, -jnp.inf)
        inter = b_c + m_prev
        m_t = jnp.maximum(inter, jnp.max(dmat, axis=-1, keepdims=True))
        wts = jnp.exp(dmat - m_t)
        sc = jnp.exp(inter - m_t)
        qk = lax.dot_general(qb, kh.astype(BF16), nt, preferred_element_type=F32) * wts
        num = _mm(qk, vb) + sc * _mm(qb, c_prev.astype(BF16))
        den = jnp.sum(qk, axis=-1, keepdims=True) + sc * jnp.sum(qh * n_prev, axis=-1, keepdims=True)
        h = num / jnp.maximum(jnp.abs(den), jnp.exp(-m_t))

        m_new = m_t[L - 1:L, :]
        b_last = b_c[L - 1:L, :]
        kw = kh * jnp.exp(b_last - b_c + li_c - m_new)
        sc_end = jnp.exp(b_last + m_prev - m_new)
        c_st[hd] = sc_end * c_prev + lax.dot_general(kw.astype(BF16), vb, tn, preferred_element_type=F32)
        n_st[hd] = sc_end * n_prev + jnp.sum(kw, axis=0, keepdims=True)
        m_st[hd] = m_new

        h = h * lax.rsqrt(jnp.mean(h * h, axis=-1, keepdims=True) + EPS)
        outs.append(h * hnw_ref[:, hd * dv:(hd + 1) * dv])

    h_ref[...] = jnp.concatenate(outs, axis=1)

    @pl.when(pl.program_id(1) == pl.num_programs(1) - 1)
    def _():
        cT_ref[0] = c_st[...]
        nT_ref[0] = n_st[...]
        mT_ref[0] = m_st[...]


def _mlstm_scan(q, k, v, gc, gr, c0, n0, m0, hn_w, B, T):
    L = min(CHUNK, T)
    nc = T // L
    H = ML_HEADS
    nq, nv = q.shape[1], v.shape[1]
    dqk, dv = nq // H, nv // H
    row = lambda w: pl.BlockSpec((L, w), lambda b, c: (b * nc + c, 0))
    st = lambda *s: pl.BlockSpec((1,) + s, lambda b, c: (b,) + (0,) * len(s))
    return pl.pallas_call(
        _mlstm_scan_kernel,
        grid=(B, nc),
        in_specs=[row(nq), row(nq), row(nv), row(LANES),
                  pl.BlockSpec((1, GATE_ROWS, L), lambda b, c: (b * nc + c, 0, 0)),
                  st(H, dqk, dv), st(H, 1, dqk), st(H, 1, 1), _const_spec(hn_w.shape)],
        out_specs=[row(nv), st(H, dqk, dv), st(H, 1, dqk), st(H, 1, 1)],
        out_shape=[jax.ShapeDtypeStruct((B * T, nv), F32), jax.ShapeDtypeStruct((B, H, dqk, dv), F32),
                   jax.ShapeDtypeStruct((B, H, 1, dqk), F32), jax.ShapeDtypeStruct((B, H, 1, 1), F32)],
        scratch_shapes=[pltpu.VMEM((H, dqk, dv), F32), pltpu.VMEM((H, 1, dqk), F32),
                        pltpu.VMEM((H, 1, 1), F32)],
        compiler_params=_params(("parallel", "arbitrary")),
        name="mlstm_scan",
    )(q, k, v, gc, gr, c0, n0.reshape(B, H, 1, dqk), m0.reshape(B, H, 1, 1), hn_w)


def _prep_weights(w):
    D = w['norm_g'].shape[-1]
    H = D // RW_HEAD
    row = lambda a: a.reshape(1, -1).astype(F32)
    head_of_lane = jnp.arange(D) // RW_HEAD
    hsum = (head_of_lane[:, None] == jnp.arange(LANES)[None, :]).astype(BF16)
    hexp = hsum.T
    layers = []
    for i in range(DEPTH):
        j = i // N_MIXERS
        ng = w['norm_g'][i]
        lp = {'g0': row(ng[0]), 'g_mix': row(ng[1]), 'g_ffn_pre': row(ng[2]), 'g_ffn_post': row(ng[3]),
              'ffn_w_gu': w['ffn_w_gu'][i].astype(BF16), 'ffn_w_down': w['ffn_w_down'][i].astype(BF16)}
        if i % N_MIXERS == 0:
            lp.update({
                'mu': jnp.pad(w['rw_mu'][j], ((0, 2), (0, 0))),
                'w_rkv': w['rw_w_rkv'][j].astype(BF16),
                'w0': row(w['rw_w0'][j]), 'w1': w['rw_w1'][j].astype(BF16), 'w2': w['rw_w2'][j].astype(BF16),
                'a0': row(w['rw_a0'][j]), 'a1': w['rw_a1'][j].astype(BF16), 'a2': w['rw_a2'][j].astype(BF16),
                'g1': w['rw_g1'][j].astype(BF16), 'g2': w['rw_g2'][j].astype(BF16),
                'k_k': row(w['rw_k_k'][j]), 'k_a': row(w['rw_k_a'][j]), 'r_k': row(w['rw_r_k'][j]),
                'ln_w': row(w['rw_ln_w'][j]), 'ln_b': row(w['rw_ln_b'][j]),
                'w_o': w['rw_w_o'][j].astype(BF16), 'hsum': hsum, 'hexp': hexp,
            })
            if j > 0:
                lp.update({'v0': row(w['rw_v0'][j - 1]), 'v1': w['rw_v1'][j - 1].astype(BF16),
                           'v2': w['rw_v2'][j - 1].astype(BF16)})
        else:
            w_in = w['ml_w_in'][j]
            n_main = w_in.shape[1] - 2 * ML_HEADS
            w_g = w_in[:, n_main:]
            bias = w['ml_b_gates'][j].astype(F32)
            lp.update({
                'w_in': w_in[:, :n_main].astype(BF16),
                'w_gc': jnp.pad(w_g, ((0, 0), (0, LANES - 2 * ML_HEADS))).astype(BF16),
                'w_gr': jnp.pad(w_g.T, ((0, GATE_ROWS - 2 * ML_HEADS), (0, 0))).astype(BF16),
                'b_c': jnp.pad(bias, (0, LANES - 2 * ML_HEADS)).reshape(1, LANES),
                'b_r': jnp.pad(bias, (0, GATE_ROWS - 2 * ML_HEADS)).reshape(GATE_ROWS, 1),
                'hn_w': row(w['ml_hn_w'][j]), 'w_out': w['ml_w_out'][j].astype(BF16),
            })
        layers.append(lp)
    return layers


def _trunk(x, rw_shift, rw_S, ml_C, ml_n, ml_m, layers):
    B, T, D = x.shape
    L = min(CHUNK, T)
    flat = lambda a: a.reshape(B * T, a.shape[-1])
    v_first = None
    shifts, Ss, Cs, ns, ms = [], [], [], [], []
    xf = flat(x)
    for i, lp in enumerate(layers):
        j = i // N_MIXERS
        if i % N_MIXERS == 0:
            rt, at, bt, kt, v, g, pt, sh = _rwkv_pre(xf.reshape(B, T, D), rw_shift[j], lp, j, v_first)
            if j == 0:
                v_first = v
            o, S = _rwkv_scan(rt, at, bt, kt, v, pt, rw_S[j], lp)
            shifts.append(sh)
            Ss.append(S)
            xf = _gated_out(xf, flat(o), flat(g), lp['w_o'], lp['g_mix'])
        else:
            q, k, v, og, gc, gr = _mlstm_pre(xf, L, lp)
            hn, C, n, m = _mlstm_scan(q, k, v, gc, gr, ml_C[j], ml_n[j], ml_m[j], lp['hn_w'], B, T)
            Cs.append(C)
            ns.append(n.reshape(B, ML_HEADS, -1))
            ms.append(m.reshape(B, ML_HEADS))
            xf = _gated_out(xf, og, hn, lp['w_out'], lp['g_mix'])
        xf = _ffn(xf, lp['g_ffn_pre'], lp['ffn_w_gu'], lp['ffn_w_down'], lp['g_ffn_post'])
    return xf.reshape(B, T, D), jnp.stack(shifts), jnp.stack(Ss), jnp.stack(Cs), jnp.stack(ns), jnp.stack(ms)


def kernel(x_prompt, x_sample, state_rwkv_shift, state_rwkv_S, state_mlstm_C, state_mlstm_n, state_mlstm_m,
           norm_g, rw_mu, rw_w_rkv, rw_w0, rw_w1, rw_w2, rw_a0, rw_a1, rw_a2, rw_v0, rw_v1, rw_v2,
           rw_g1, rw_g2, rw_k_k, rw_k_a, rw_r_k, rw_ln_w, rw_ln_b, rw_w_o,
           ml_w_in, ml_b_gates, ml_hn_w, ml_w_out, ffn_w_gu, ffn_w_down):
    w = {
        'norm_g': norm_g, 'rw_mu': rw_mu, 'rw_w_rkv': rw_w_rkv, 'rw_w0': rw_w0, 'rw_w1': rw_w1,
        'rw_w2': rw_w2, 'rw_a0': rw_a0, 'rw_a1': rw_a1, 'rw_a2': rw_a2, 'rw_v0': rw_v0, 'rw_v1': rw_v1,
        'rw_v2': rw_v2, 'rw_g1': rw_g1, 'rw_g2': rw_g2, 'rw_k_k': rw_k_k, 'rw_k_a': rw_k_a,
        'rw_r_k': rw_r_k, 'rw_ln_w': rw_ln_w, 'rw_ln_b': rw_ln_b, 'rw_w_o': rw_w_o,
        'ml_w_in': ml_w_in, 'ml_b_gates': ml_b_gates, 'ml_hn_w': ml_hn_w, 'ml_w_out': ml_w_out,
        'ffn_w_gu': ffn_w_gu, 'ffn_w_down': ffn_w_down,
    }
    layers = _prep_weights(w)
    Bp = x_prompt.shape[0]
    zeros = lambda a: jnp.zeros((a.shape[0], Bp) + a.shape[2:], F32)
    out_p = _trunk(x_prompt, zeros(state_rwkv_shift), zeros(state_rwkv_S), zeros(state_mlstm_C),
                   zeros(state_mlstm_n), zeros(state_mlstm_m), layers)
    out_s = _trunk(x_sample, state_rwkv_shift, state_rwkv_S, state_mlstm_C, state_mlstm_n, state_mlstm_m,
                   layers)
    return (out_p[0], out_s[0]) + tuple(out_p[1:]) + tuple(out_s[1:])
```

```python
import functools
import math

import jax
import jax.numpy as jnp
from jax import lax
from jax.experimental import pallas as pl
from jax.experimental.pallas import tpu as pltpu

F32 = jnp.float32
BF16 = jnp.bfloat16

DEPTH = 4
N_MIXERS = 2
CHUNK = 64
RW_HEAD = 64
ML_HEADS = 4
GN_EPS = 64e-5
GATE_CAP = 15.0
EPS = 1e-6

LANES = 128
VMEM_LIMIT = 56 * 1024 * 1024
TOKEN_TILE = 256
MAPS_TILE = 256
MAPS_LANES = 256
STATE_TILE = 256
MLSTM_TILE = 256
GATE_ROWS = 16


def _const_spec(shape):
    nd = len(shape)
    return pl.BlockSpec(shape, lambda *_: (0,) * nd, pipeline_mode=pl.Buffered(1))


def _params(sem, vmem=VMEM_LIMIT):
    return pltpu.CompilerParams(dimension_semantics=sem, vmem_limit_bytes=vmem)


def _rms(x, g):
    return x * lax.rsqrt(jnp.mean(x * x, axis=-1, keepdims=True) + EPS) * g


def _mm(a, w):
    return jnp.dot(a.astype(BF16), w, preferred_element_type=F32)


def _split2(x):
    hi = x.astype(BF16)
    lo = (x - hi.astype(F32)).astype(BF16)
    return hi, lo


def _split3(x):
    hi = x.astype(BF16)
    r1 = x - hi.astype(F32)
    mid = r1.astype(BF16)
    lo = (r1 - mid.astype(F32)).astype(BF16)
    return hi, mid, lo


def _dot_exact_rhs(x, w01):
    hi, lo = _split2(x)
    return (jnp.dot(hi, w01, preferred_element_type=F32)
            + jnp.dot(lo, w01, preferred_element_type=F32))


def _dot_exact_lhs(w01, x):
    hi, mid, lo = _split3(x)
    return (jnp.dot(w01, hi, preferred_element_type=F32)
            + jnp.dot(w01, mid, preferred_element_type=F32)
            + jnp.dot(w01, lo, preferred_element_type=F32))


def _rwkv_pre_kernel(has_vfirst, nct, *refs):
    (x_ref, shift0_ref, g0_ref, mu_ref, wrkv_ref, w0_ref, w1_ref, w2_ref, a0_ref, a1_ref, a2_ref,
     g1_ref, g2_ref, kk_ref, ka_ref, hsum_ref, hexp_ref, tri_ref, sel_ref) = refs[:19]
    refs = refs[19:]
    if has_vfirst:
        v0_ref, v1_ref, v2_ref, vfirst_ref = refs[:4]
        refs = refs[4:]
    rt_ref, at_ref, bt_ref, kt_ref, v_ref, g_ref, pt_ref, shift_ref, carry = refs

    tm = x_ref.shape[1]

    @pl.when(pl.program_id(1) == 0)
    def _():
        carry[...] = shift0_ref[0]

    h = _rms(x_ref[0], g0_ref[...])
    row = lax.broadcasted_iota(jnp.int32, h.shape, 0)
    prev = jnp.where(row == 0, carry[...], pltpu.roll(h, 1, axis=0))
    last = h[tm - 1:tm, :]
    carry[...] = last
    shift_ref[0] = last

    dx = prev - h
    xr, xw, xk, xv, xa, xg = (h + dx * mu_ref[c:c + 1, :] for c in range(6))
    r = _mm(xr, wrkv_ref[0])
    k = _mm(xk, wrkv_ref[1])
    v = _mm(xv, wrkv_ref[2])
    wl = w0_ref[...] + _mm(jnp.tanh(_mm(xw, w1_ref[...])), w2_ref[...])
    lw = (-math.exp(-0.5)) * jax.nn.sigmoid(wl)
    a = jax.nn.sigmoid(a0_ref[...] + _mm(_mm(xa, a1_ref[...]), a2_ref[...]))
    g = _mm(jax.nn.sigmoid(_mm(xg, g1_ref[...])), g2_ref[...])
    if has_vfirst:
        mix = jax.nn.sigmoid(v0_ref[...] + _mm(_mm(xv, v1_ref[...]), v2_ref[...]))
        v = v + (vfirst_ref[0] - v) * mix

    kkr = k * kk_ref[...]
    ssq = _dot_exact_rhs(kkr * kkr, hsum_ref[...])
    inv = lax.rsqrt(jnp.maximum(ssq, 1e-24))
    kk = kkr * _dot_exact_rhs(inv, hexp_ref[...])
    kmod = k * (1.0 + (a - 1.0) * ka_ref[...])

    b = _dot_exact_lhs(tri_ref[...], lw)
    eb = jnp.exp(b)
    enb = jnp.exp(-b)
    rt_ref[0] = r * eb
    at_ref[0] = -kk * jnp.exp(b - lw)
    bt_ref[0] = kk * a * enb
    kt_ref[0] = kmod * enb
    v_ref[0] = v
    g_ref[0] = g
    pt_ref[0, 0] = jnp.exp(_dot_exact_lhs(sel_ref[...], lw))[:nct]


def _rwkv_pre(x, shift0, p, j, v_first):
    B, T, D = x.shape
    L = min(CHUNK, T)
    tm = min(TOKEN_TILE, T)
    nct = tm // L
    has_vfirst = v_first is not None
    row = jnp.arange(tm)
    tri = ((row[:, None] >= row[None, :]) & (row[:, None] // L == row[None, :] // L)).astype(BF16)
    sel = (jnp.arange(GATE_ROWS)[:, None] == row[None, :] // L).astype(BF16)

    tile = pl.BlockSpec((1, tm, D), lambda b, t: (b, t, 0))
    consts = [p['g0'], p['mu'], p['w_rkv'], p['w0'], p['w1'], p['w2'], p['a0'], p['a1'], p['a2'],
              p['g1'], p['g2'], p['k_k'], p['k_a'], p['hsum'], p['hexp'], tri, sel]
    args = [x, shift0.reshape(B, 1, D)] + consts
    specs = [tile, pl.BlockSpec((1, 1, D), lambda b, t: (b, 0, 0))] + [_const_spec(c.shape) for c in consts]
    if has_vfirst:
        vc = [p['v0'], p['v1'], p['v2']]
        args += vc + [v_first]
        specs += [_const_spec(c.shape) for c in vc] + [tile]

    big = jax.ShapeDtypeStruct((B, T, D), F32)
    outs = pl.pallas_call(
        functools.partial(_rwkv_pre_kernel, has_vfirst, nct),
        grid=(B, T // tm),
        in_specs=specs,
        out_specs=[tile] * 6 + [pl.BlockSpec((1, 1, nct, D), lambda b, t: (b, t, 0, 0)),
                                pl.BlockSpec((1, 1, D), lambda b, t: (b, 0, 0))],
        out_shape=[big] * 6 + [jax.ShapeDtypeStruct((B, T // tm, nct, D), F32),
                               jax.ShapeDtypeStruct((B, 1, D), F32)],
        scratch_shapes=[pltpu.VMEM((1, D), F32)],
        compiler_params=_params(("parallel", "arbitrary")),
        name="rwkv_pre",
    )(*args)
    rt, at, bt, kt, v, g, pt, shift = outs
    return rt, at, bt, kt, v, g, pt.reshape(B, T // L, 1, D), shift.reshape(B, D)


def _rwkv_chunk_maps(L, items):
    N = RW_HEAD
    nt = (((1,), (1,)), ((), ()))
    tn = (((0,), (0,)), ((), ()))
    row = lax.broadcasted_iota(jnp.int32, (L, 2 * L), 0)
    col = lax.broadcasted_iota(jnp.int32, (L, 2 * L), 1)
    col = jnp.where(col >= L, col - L, col)
    strict = row > col
    incl = row >= col
    zeros = jnp.zeros((L, N), F32)

    gs = [lax.dot_general(jnp.concatenate([at, rt], axis=0).astype(BF16),
                          jnp.concatenate([bt, kt], axis=0).astype(BF16), nt, preferred_element_type=F32)
          for at, rt, bt, kt, _, _ in items]
    tops = [jnp.where(strict, g[:L], 0.0) for g in gs]
    bots = [jnp.where(incl, g[L:], 0.0) for g in gs]
    wus = [jnp.concatenate([it[0], _mm(top, jnp.concatenate([zeros, it[4]], axis=0).astype(BF16))], axis=1)
           for it, top in zip(items, tops)]
    n_pows = [top[:, :L] for top in tops]
    span = 1
    while True:
        span *= 2
        if span >= L:
            wus = [wu + _mm(n_pow, wu.astype(BF16)) for wu, n_pow in zip(wus, n_pows)]
            break
        res = [_mm(n_pow, jnp.concatenate([wu, n_pow], axis=1).astype(BF16))
               for wu, n_pow in zip(wus, n_pows)]
        wus = [wu + r[:, :2 * N] for wu, r in zip(wus, res)]
        n_pows = [r[:, 2 * N:] for r in res]

    zs = [jnp.concatenate([wu, jnp.concatenate([zeros, it[4]], axis=1)], axis=0).astype(BF16)
          for it, wu in zip(items, wus)]
    ros = [_mm(bot, z) for bot, z in zip(bots, zs)]
    mcs = [lax.dot_general(z, jnp.concatenate([it[2] * it[5], it[3] * it[5]], axis=0).astype(BF16), tn,
                           preferred_element_type=F32)
           for it, z in zip(items, zs)]
    return [(it[1] + ro[:, :N], ro[:, N:], mc[:N], mc[N:]) for it, ro, mc in zip(items, ros, mcs)]


def _rwkv_maps_kernel(L, rt_ref, at_ref, bt_ref, kt_ref, v_ref, pt_ref, rk_ref,
                      rhat_ref, ohat_ref, bonus_ref, m_ref, c_ref):
    N = RW_HEAD
    nck = rt_ref.shape[1] // L
    heads = rt_ref.shape[2] // N
    where = [(slice(c * L, (c + 1) * L), slice(i * N, (i + 1) * N), c)
             for c in range(nck) for i in range(heads)]
    items = [(at_ref[0, rows, sl], rt_ref[0, rows, sl], bt_ref[0, rows, sl], kt_ref[0, rows, sl],
              v_ref[0, rows, sl], pt_ref[0, c, :, sl]) for rows, sl, c in where]
    maps = _rwkv_chunk_maps(L, items)
    for (rows, sl, c), it, (rhat, ohat, m, cc) in zip(where, items, maps):
        _, rt, _, kt, vh, _ = it
        rhat_ref[0, rows, sl] = rhat.astype(BF16)
        ohat_ref[0, rows, sl] = ohat
        bonus_ref[0, rows, sl] = jnp.sum(rt * kt * rk_ref[:, sl], axis=-1, keepdims=True) * vh
        m_ref[0, c, :, sl] = m.astype(BF16)
        c_ref[0, c, :, sl] = cc


def _rwkv_maps(rt, at, bt, kt, v, pt, r_k):
    B, T, D = rt.shape
    L = min(CHUNK, T)
    ts = min(MAPS_TILE, T)
    nck = ts // L
    N = RW_HEAD
    tile = pl.BlockSpec((1, ts, MAPS_LANES), lambda b, h, c: (b, c, h))
    sq = pl.BlockSpec((1, nck, N, MAPS_LANES), lambda b, h, c: (b, c, 0, h))
    big = lambda dt: jax.ShapeDtypeStruct((B, T, D), dt)
    return pl.pallas_call(
        functools.partial(_rwkv_maps_kernel, L),
        grid=(B, D // MAPS_LANES, T // ts),
        in_specs=[tile] * 5 + [pl.BlockSpec((1, nck, 1, MAPS_LANES), lambda b, h, c: (b, c, 0, h)),
                               pl.BlockSpec((1, MAPS_LANES), lambda b, h, c: (0, h))],
        out_specs=[tile, tile, tile, sq, sq],
        out_shape=[big(BF16), big(F32), big(F32),
                   jax.ShapeDtypeStruct((B, T // L, N, D), BF16), jax.ShapeDtypeStruct((B, T // L, N, D), F32)],
        compiler_params=_params(("parallel", "parallel", "parallel")),
        name="rwkv_maps",
    )(rt, at, bt, kt, v, pt, r_k)


def _rwkv_state_kernel(L, rhat_ref, ohat_ref, bonus_ref, m_ref, c_ref, pt_ref, s0_ref, lnw_ref, lnb_ref,
                       o_ref, sT_ref, state):
    N = RW_HEAD
    nck = rhat_ref.shape[1] // L
    npairs = rhat_ref.shape[2] // LANES

    @pl.when(pl.program_id(1) == 0)
    def _():
        state[...] = s0_ref[0]

    nt = (((1,), (1,)), ((), ()))
    lane = lax.broadcasted_iota(jnp.int32, (1, LANES), 1)
    first = lane < N
    zero = jnp.zeros((), BF16)

    def block_diag(a):
        return jnp.concatenate([jnp.where(first, a, zero), jnp.where(first, zero, a)], axis=0)

    def head_sum(a):
        total = jnp.sum(a, axis=-1, keepdims=True)
        part = jnp.sum(jnp.where(first, a, 0.0), axis=-1, keepdims=True)
        return jnp.where(first, part, total - part)

    s = [state[:, p * LANES:(p + 1) * LANES] for p in range(npairs)]
    for c in range(nck):
        rows = slice(c * L, (c + 1) * L)
        os = []
        for p in range(npairs):
            pl_ = slice(p * LANES, (p + 1) * LANES)
            sb = s[p].astype(BF16)
            os.append(lax.dot_general(rhat_ref[0, rows, pl_], block_diag(sb), nt, preferred_element_type=F32)
                      + ohat_ref[0, rows, pl_])
            s[p] = (s[p] * pt_ref[0, c, :, pl_]
                    + jnp.dot(sb, block_diag(m_ref[0, c, :, pl_]), preferred_element_type=F32)
                    + c_ref[0, c, :, pl_])
        ds = [o - m for o, m in zip(os, [head_sum(o) * (1.0 / N) for o in os])]
        variances = [head_sum(d * d) * (1.0 / N) for d in ds]
        for p, (d, var) in enumerate(zip(ds, variances)):
            pl_ = slice(p * LANES, (p + 1) * LANES)
            o_ref[0, rows, pl_] = (d * lax.rsqrt(var + GN_EPS) * lnw_ref[:, pl_] + lnb_ref[:, pl_]
                                   + bonus_ref[0, rows, pl_])
    for p in range(npairs):
        state[:, p * LANES:(p + 1) * LANES] = s[p]

    @pl.when(pl.program_id(1) == pl.num_programs(1) - 1)
    def _():
        sT_ref[0] = state[...]


def _rwkv_state(rhat, ohat, bonus, m, c, pt, s0, p):
    B, T, D = ohat.shape
    L = min(CHUNK, T)
    ts = min(STATE_TILE, T)
    nck = ts // L
    N = RW_HEAD
    H = D // N
    tile = pl.BlockSpec((1, ts, D), lambda b, t: (b, t, 0))
    sq = pl.BlockSpec((1, nck, N, D), lambda b, t: (b, t, 0, 0))
    st = pl.BlockSpec((1, N, D), lambda b, t: (b, 0, 0))
    s0_vk = jnp.transpose(s0, (0, 2, 1, 3)).reshape(B, N, D)
    o, sT = pl.pallas_call(
        functools.partial(_rwkv_state_kernel, L),
        grid=(B, T // ts),
        in_specs=[tile, tile, tile, sq, sq, pl.BlockSpec((1, nck, 1, D), lambda b, t: (b, t, 0, 0)), st,
                  _const_spec((1, D)), _const_spec((1, D))],
        out_specs=[tile, st],
        out_shape=[jax.ShapeDtypeStruct((B, T, D), F32), jax.ShapeDtypeStruct((B, N, D), F32)],
        scratch_shapes=[pltpu.VMEM((N, D), F32)],
        compiler_params=_params(("parallel", "arbitrary")),
        name="rwkv_state",
    )(rhat, ohat, bonus, m, c, pt, s0_vk, p['ln_w'], p['ln_b'])
    return o, jnp.transpose(sT.reshape(B, N, H, N), (0, 2, 1, 3))


def _rwkv_scan(rt, at, bt, kt, v, pt, s0, p):
    rhat, ohat, bonus, m, c = _rwkv_maps(rt, at, bt, kt, v, pt, p['r_k'])
    return _rwkv_state(rhat, ohat, bonus, m, c, pt, s0, p)


def _gated_out_kernel(x_ref, a_ref, b_ref, w_ref, g_ref, o_ref):
    y = _mm(a_ref[...] * b_ref[...], w_ref[...])
    o_ref[...] = x_ref[...] + _rms(y, g_ref[...])


def _gated_out(x, a, b, w, g):
    R, D = x.shape
    tm = min(TOKEN_TILE, R)
    tile = pl.BlockSpec((tm, D), lambda t: (t, 0))
    return pl.pallas_call(
        _gated_out_kernel,
        grid=(R // tm,),
        in_specs=[tile, tile, tile, _const_spec(w.shape), _const_spec(g.shape)],
        out_specs=tile,
        out_shape=jax.ShapeDtypeStruct((R, D), F32),
        compiler_params=_params(("parallel",)),
        name="gated_out",
    )(x, a, b, w, g)


def _ffn_kernel(x_ref, gpre_ref, wgu_ref, wd_ref, gpost_ref, o_ref):
    x = x_ref[...]
    f = wd_ref.shape[0]
    gu = _mm(_rms(x, gpre_ref[...]), wgu_ref[...])
    gate = gu[:, :f]
    act = gate * jax.nn.sigmoid(gate) * gu[:, f:]
    o_ref[...] = x + _rms(_mm(act, wd_ref[...]), gpost_ref[...])


def _ffn(x, gpre, wgu, wd, gpost):
    R, D = x.shape
    tm = min(TOKEN_TILE, R)
    tile = pl.BlockSpec((tm, D), lambda t: (t, 0))
    return pl.pallas_call(
        _ffn_kernel,
        grid=(R // tm,),
        in_specs=[tile] + [_const_spec(c.shape) for c in (gpre, wgu, wd, gpost)],
        out_specs=tile,
        out_shape=jax.ShapeDtypeStruct((R, D), F32),
        compiler_params=_params(("parallel",)),
        name="ffn",
    )(x, gpre, wgu, wd, gpost)


def _mlstm_pre_kernel(L, nq, x_ref, g0_ref, win_ref, wgc_ref, wgr_ref, bc_ref, br_ref,
                      q_ref, k_ref, v_ref, og_ref, gc_ref, gr_ref):
    h = _rms(x_ref[...], g0_ref[...]).astype(BF16)
    z = jnp.dot(h, win_ref[...], preferred_element_type=F32)
    nv = v_ref.shape[1]
    dqk = nq // ML_HEADS
    q_ref[...] = z[:, :nq] * (dqk ** -0.5)
    k_ref[...] = z[:, nq:2 * nq]
    v_ref[...] = z[:, 2 * nq:2 * nq + nv]
    og_ref[...] = jax.nn.sigmoid(z[:, 2 * nq + nv:])

    def cap(t):
        return GATE_CAP * jnp.tanh(t / GATE_CAP)

    gc = cap(jnp.dot(h, wgc_ref[...], preferred_element_type=F32) + bc_ref[...])
    lane = lax.broadcasted_iota(jnp.int32, gc.shape, 1)
    gc_ref[...] = jnp.where(lane < ML_HEADS, gc, jax.nn.log_sigmoid(gc))
    gr = cap(lax.dot_general(wgr_ref[...], h, (((1,), (1,)), ((), ())), preferred_element_type=F32)
             + br_ref[...])
    rowi = lax.broadcasted_iota(jnp.int32, gr.shape, 0)
    gr = jnp.where(rowi < ML_HEADS, gr, jax.nn.log_sigmoid(gr))
    for c in range(gr_ref.shape[0]):
        gr_ref[c] = gr[:, c * L:(c + 1) * L]


def _mlstm_pre(x, L, p):
    R, D = x.shape
    tm = min(TOKEN_TILE, R)
    nq = ML_HEADS * (D // (2 * ML_HEADS))
    nv = D
    consts = [p['g0'], p['w_in'], p['w_gc'], p['w_gr'], p['b_c'], p['b_r']]
    row = lambda w: pl.BlockSpec((tm, w), lambda t: (t, 0))
    return pl.pallas_call(
        functools.partial(_mlstm_pre_kernel, L, nq),
        grid=(R // tm,),
        in_specs=[row(D)] + [_const_spec(c.shape) for c in consts],
        out_specs=[row(nq), row(nq), row(nv), row(nv), row(LANES),
                   pl.BlockSpec((tm // L, GATE_ROWS, L), lambda t: (t, 0, 0))],
        out_shape=[jax.ShapeDtypeStruct((R, nq), F32), jax.ShapeDtypeStruct((R, nq), F32),
                   jax.ShapeDtypeStruct((R, nv), F32), jax.ShapeDtypeStruct((R, nv), F32),
                   jax.ShapeDtypeStruct((R, LANES), F32),
                   jax.ShapeDtypeStruct((R // L, GATE_ROWS, L), F32)],
        compiler_params=_params(("parallel",)),
        name="mlstm_pre",
    )(x, *consts)


def _mlstm_scan_kernel(L, q_ref, k_ref, v_ref, gc_ref, gr_ref, c0_ref, n0_ref, m0_ref, hnw_ref,
                       h_ref, cT_ref, nT_ref, mT_ref, c_st, n_st, m_st):
    H = ML_HEADS
    nck = q_ref.shape[0] // L
    dqk = q_ref.shape[1] // H
    dv = v_ref.shape[1] // H

    @pl.when(pl.program_id(1) == 0)
    def _():
        c_st[...] = c0_ref[0]
        n_st[...] = n0_ref[0]
        m_st[...] = m0_ref[0]

    row = lax.broadcasted_iota(jnp.int32, (L, L), 0)
    col = lax.broadcasted_iota(jnp.int32, (L, L), 1)
    causal = row >= col
    tril = causal.astype(BF16)
    triu = (row <= col).astype(BF16)
    nt = (((1,), (1,)), ((), ()))
    tn = (((0,), (0,)), ((), ()))

    gcs = [gc_ref[c * L:(c + 1) * L, :] for c in range(nck)]
    grs = [gr_ref[c] for c in range(nck)]
    bcs = [_dot_exact_lhs(tril, gc) for gc in gcs]
    brs = []
    for gr in grs:
        hi, mid, lo = _split3(gr)
        brs.append(jnp.dot(hi, triu, preferred_element_type=F32) + jnp.dot(mid, triu, preferred_element_type=F32)
                   + jnp.dot(lo, triu, preferred_element_type=F32))

    where = [(c, hd) for c in range(nck) for hd in range(H)]
    rows_of = [slice(c * L, (c + 1) * L) for c, _ in where]
    qs = [q_ref[rows, hd * dqk:(hd + 1) * dqk] for rows, (_, hd) in zip(rows_of, where)]
    ks = [k_ref[rows, hd * dqk:(hd + 1) * dqk] for rows, (_, hd) in zip(rows_of, where)]
    vbs = [v_ref[rows, hd * dv:(hd + 1) * dv].astype(BF16) for rows, (_, hd) in zip(rows_of, where)]
    li_cs = [gcs[c][:, hd:hd + 1] for c, hd in where]
    b_cs = [bcs[c][:, H + hd:H + hd + 1] for c, hd in where]
    dmats = [jnp.where(causal, b_c - brs[c][H + hd:H + hd + 1, :] + grs[c][hd:hd + 1, :], -jnp.inf)
             for (c, hd), b_c in zip(where, b_cs)]
    intras = [jnp.max(dmat, axis=-1, keepdims=True) for dmat in dmats]
    gs = [intra - b_c for intra, b_c in zip(intras, b_cs)]
    g_lasts = [g[L - 1:L, :] for g in gs]
    b_lasts = [b_c[L - 1:L, :] for b_c in b_cs]
    es = [jnp.exp(dmat - intra) for dmat, intra in zip(dmats, intras)]
    kws = [k * jnp.exp(li_c - b_c - g_last) for k, li_c, b_c, g_last in zip(ks, li_cs, b_cs, g_lasts)]
    qbs = [q.astype(BF16) for q in qs]
    qks = [lax.dot_general(qb, k.astype(BF16), nt, preferred_element_type=F32) * e
           for qb, k, e in zip(qbs, ks, es)]
    num_i = [_mm(qk, vb) for qk, vb in zip(qks, vbs)]
    kvs = [lax.dot_general(kw.astype(BF16), vb, tn, preferred_element_type=F32) for kw, vb in zip(kws, vbs)]
    ksums = [jnp.sum(kw, axis=0, keepdims=True) for kw in kws]
    qk_sums = [jnp.sum(qk, axis=-1, keepdims=True) for qk in qks]

    c_cur = [c_st[hd] for hd in range(H)]
    n_cur = [n_st[hd] for hd in range(H)]
    m_cur = [m_st[hd] for hd in range(H)]
    before = []
    for (c, hd), g_last, b_last, kv, ksum in zip(where, g_lasts, b_lasts, kvs, ksums):
        before.append((c_cur[hd], n_cur[hd], m_cur[hd]))
        mx = jnp.maximum(m_cur[hd], g_last)
        e_old = jnp.exp(m_cur[hd] - mx)
        e_new = jnp.exp(g_last - mx)
        c_cur[hd] = e_old * c_cur[hd] + e_new * kv
        n_cur[hd] = e_old * n_cur[hd] + e_new * ksum
        m_cur[hd] = b_last + mx
    for hd in range(H):
        c_st[hd] = c_cur[hd]
        n_st[hd] = n_cur[hd]
        m_st[hd] = m_cur[hd]

    qcs = [_mm(qb, cp.astype(BF16)) for qb, (cp, _, _) in zip(qbs, before)]
    qn_sums = [jnp.sum(q * n_prev, axis=-1, keepdims=True) for q, (_, n_prev, _) in zip(qs, before)]
    mxs = [jnp.maximum(m_prev, g) for g, (_, _, m_prev) in zip(gs, before)]
    f_news = [jnp.exp(g - mx) for g, mx in zip(gs, mxs)]
    f_olds = [jnp.exp(m_prev - mx) for mx, (_, _, m_prev) in zip(mxs, before)]
    floors = [jnp.exp(-(b_c + mx)) for b_c, mx in zip(b_cs, mxs)]
    dens = [f_new * qk_sum + f_old * qn_sum
            for f_new, f_old, qk_sum, qn_sum in zip(f_news, f_olds, qk_sums, qn_sums)]
    hs = [(f_new * ni + f_old * qc) / jnp.maximum(jnp.abs(den), floor)
          for f_new, f_old, ni, qc, den, floor in zip(f_news, f_olds, num_i, qcs, dens, floors)]
    h_ms = [jnp.mean(h * h, axis=-1, keepdims=True) for h in hs]
    for rows, (_, hd), h, h_m in zip(rows_of, where, hs, h_ms):
        h_ref[rows, hd * dv:(hd + 1) * dv] = h * lax.rsqrt(h_m + EPS) * hnw_ref[:, hd * dv:(hd + 1) * dv]

    @pl.when(pl.program_id(1) == pl.num_programs(1) - 1)
    def _():
        cT_ref[0] = c_st[...]
        nT_ref[0] = n_st[...]
        mT_ref[0] = m_st[...]


def _mlstm_scan(q, k, v, gc, gr, c0, n0, m0, hn_w, B, T):
    L = min(CHUNK, T)
    ts = min(MLSTM_TILE, T)
    nc = T // ts
    H = ML_HEADS
    nq, nv = q.shape[1], v.shape[1]
    dqk, dv = nq // H, nv // H
    row = lambda w: pl.BlockSpec((ts, w), lambda b, c: (b * nc + c, 0))
    st = lambda *s: pl.BlockSpec((1,) + s, lambda b, c: (b,) + (0,) * len(s))
    return pl.pallas_call(
        functools.partial(_mlstm_scan_kernel, L),
        grid=(B, nc),
        in_specs=[row(nq), row(nq), row(nv), row(LANES),
                  pl.BlockSpec((ts // L, GATE_ROWS, L), lambda b, c: (b * nc + c, 0, 0)),
                  st(H, dqk, dv), st(H, 1, dqk), st(H, 1, 1), _const_spec(hn_w.shape)],
        out_specs=[row(nv), st(H, dqk, dv), st(H, 1, dqk), st(H, 1, 1)],
        out_shape=[jax.ShapeDtypeStruct((B * T, nv), F32), jax.ShapeDtypeStruct((B, H, dqk, dv), F32),
                   jax.ShapeDtypeStruct((B, H, 1, dqk), F32), jax.ShapeDtypeStruct((B, H, 1, 1), F32)],
        scratch_shapes=[pltpu.VMEM((H, dqk, dv), F32), pltpu.VMEM((H, 1, dqk), F32),
                        pltpu.VMEM((H, 1, 1), F32)],
        compiler_params=_params(("parallel", "arbitrary")),
        name="mlstm_scan",
    )(q, k, v, gc, gr, c0, n0.reshape(B, H, 1, dqk), m0.reshape(B, H, 1, 1), hn_w)


def _prep_weights(w):
    D = w['norm_g'].shape[-1]
    H = D // RW_HEAD
    row = lambda a: a.reshape(1, -1).astype(F32)
    head_of_lane = jnp.arange(D) // RW_HEAD
    hsum = (head_of_lane[:, None] == jnp.arange(LANES)[None, :]).astype(BF16)
    hexp = hsum.T
    layers = []
    for i in range(DEPTH):
        j = i // N_MIXERS
        ng = w['norm_g'][i]
        lp = {'g0': row(ng[0]), 'g_mix': row(ng[1]), 'g_ffn_pre': row(ng[2]), 'g_ffn_post': row(ng[3]),
              'ffn_w_gu': w['ffn_w_gu'][i].astype(BF16), 'ffn_w_down': w['ffn_w_down'][i].astype(BF16)}
        if i % N_MIXERS == 0:
            lp.update({
                'mu': jnp.pad(w['rw_mu'][j], ((0, 2), (0, 0))),
                'w_rkv': w['rw_w_rkv'][j].astype(BF16),
                'w0': row(w['rw_w0'][j]), 'w1': w['rw_w1'][j].astype(BF16), 'w2': w['rw_w2'][j].astype(BF16),
                'a0': row(w['rw_a0'][j]), 'a1': w['rw_a1'][j].astype(BF16), 'a2': w['rw_a2'][j].astype(BF16),
                'g1': w['rw_g1'][j].astype(BF16), 'g2': w['rw_g2'][j].astype(BF16),
                'k_k': row(w['rw_k_k'][j]), 'k_a': row(w['rw_k_a'][j]), 'r_k': row(w['rw_r_k'][j]),
                'ln_w': row(w['rw_ln_w'][j]), 'ln_b': row(w['rw_ln_b'][j]),
                'w_o': w['rw_w_o'][j].astype(BF16), 'hsum': hsum, 'hexp': hexp,
            })
            if j > 0:
                lp.update({'v0': row(w['rw_v0'][j - 1]), 'v1': w['rw_v1'][j - 1].astype(BF16),
                           'v2': w['rw_v2'][j - 1].astype(BF16)})
        else:
            w_in = w['ml_w_in'][j]
            n_main = w_in.shape[1] - 2 * ML_HEADS
            w_g = w_in[:, n_main:]
            bias = w['ml_b_gates'][j].astype(F32)
            lp.update({
                'w_in': w_in[:, :n_main].astype(BF16),
                'w_gc': jnp.pad(w_g, ((0, 0), (0, LANES - 2 * ML_HEADS))).astype(BF16),
                'w_gr': jnp.pad(w_g.T, ((0, GATE_ROWS - 2 * ML_HEADS), (0, 0))).astype(BF16),
                'b_c': jnp.pad(bias, (0, LANES - 2 * ML_HEADS)).reshape(1, LANES),
                'b_r': jnp.pad(bias, (0, GATE_ROWS - 2 * ML_HEADS)).reshape(GATE_ROWS, 1),
                'hn_w': row(w['ml_hn_w'][j]), 'w_out': w['ml_w_out'][j].astype(BF16),
            })
        layers.append(lp)
    return layers


def _trunk(x, rw_shift, rw_S, ml_C, ml_n, ml_m, layers):
    B, T, D = x.shape
    L = min(CHUNK, T)
    flat = lambda a: a.reshape(B * T, a.shape[-1])
    v_first = None
    shifts, Ss, Cs, ns, ms = [], [], [], [], []
    xf = flat(x)
    for i, lp in enumerate(layers):
        j = i // N_MIXERS
        if i % N_MIXERS == 0:
            rt, at, bt, kt, v, g, pt, sh = _rwkv_pre(xf.reshape(B, T, D), rw_shift[j], lp, j, v_first)
            if j == 0:
                v_first = v
            o, S = _rwkv_scan(rt, at, bt, kt, v, pt, rw_S[j], lp)
            shifts.append(sh)
            Ss.append(S)
            xf = _gated_out(xf, flat(o), flat(g), lp['w_o'], lp['g_mix'])
        else:
            q, k, v, og, gc, gr = _mlstm_pre(xf, L, lp)
            hn, C, n, m = _mlstm_scan(q, k, v, gc, gr, ml_C[j], ml_n[j], ml_m[j], lp['hn_w'], B, T)
            Cs.append(C)
            ns.append(n.reshape(B, ML_HEADS, -1))
            ms.append(m.reshape(B, ML_HEADS))
            xf = _gated_out(xf, og, hn, lp['w_out'], lp['g_mix'])
        xf = _ffn(xf, lp['g_ffn_pre'], lp['ffn_w_gu'], lp['ffn_w_down'], lp['g_ffn_post'])
    return xf.reshape(B, T, D), jnp.stack(shifts), jnp.stack(Ss), jnp.stack(Cs), jnp.stack(ns), jnp.stack(ms)


def kernel(x_prompt, x_sample, state_rwkv_shift, state_rwkv_S, state_mlstm_C, state_mlstm_n, state_mlstm_m,
           norm_g, rw_mu, rw_w_rkv, rw_w0, rw_w1, rw_w2, rw_a0, rw_a1, rw_a2, rw_v0, rw_v1, rw_v2,
           rw_g1, rw_g2, rw_k_k, rw_k_a, rw_r_k, rw_ln_w, rw_ln_b, rw_w_o,
           ml_w_in, ml_b_gates, ml_hn_w, ml_w_out, ffn_w_gu, ffn_w_down):
    w = {
        'norm_g': norm_g, 'rw_mu': rw_mu, 'rw_w_rkv': rw_w_rkv, 'rw_w0': rw_w0, 'rw_w1': rw_w1,
        'rw_w2': rw_w2, 'rw_a0': rw_a0, 'rw_a1': rw_a1, 'rw_a2': rw_a2, 'rw_v0': rw_v0, 'rw_v1': rw_v1,
        'rw_v2': rw_v2, 'rw_g1': rw_g1, 'rw_g2': rw_g2, 'rw_k_k': rw_k_k, 'rw_k_a': rw_k_a,
        'rw_r_k': rw_r_k, 'rw_ln_w': rw_ln_w, 'rw_ln_b': rw_ln_b, 'rw_w_o': rw_w_o,
        'ml_w_in': ml_w_in, 'ml_b_gates': ml_b_gates, 'ml_hn_w': ml_hn_w, 'ml_w_out': ml_w_out,
        'ffn_w_gu': ffn_w_gu, 'ffn_w_down': ffn_w_down,
    }
    layers = _prep_weights(w)
    Bp = x_prompt.shape[0]
    zeros = lambda a: jnp.zeros((a.shape[0], Bp) + a.shape[2:], F32)
    out_p = _trunk(x_prompt, zeros(state_rwkv_shift), zeros(state_rwkv_S), zeros(state_mlstm_C),
                   zeros(state_mlstm_n), zeros(state_mlstm_m), layers)
    out_s = _trunk(x_sample, state_rwkv_shift, state_rwkv_S, state_mlstm_C, state_mlstm_n, state_mlstm_m,
                   layers)
    return (out_p[0], out_s[0]) + tuple(out_p[1:]) + tuple(out_s[1:])
```

```python
import functools
import math

import jax
import jax.numpy as jnp
from jax import lax
from jax.experimental import pallas as pl
from jax.experimental.pallas import tpu as pltpu

F32 = jnp.float32
BF16 = jnp.bfloat16

DEPTH = 4
N_MIXERS = 2
CHUNK = 64
RW_HEAD = 64
ML_HEADS = 4
GN_EPS = 64e-5
GATE_CAP = 15.0
EPS = 1e-6

LANES = 128
VMEM_LIMIT = 56 * 1024 * 1024
TOKEN_TILE = 256
MAPS_TILE = 512
MAPS_LANES = 256
STATE_TILE = 256
MLSTM_TILE = 256
GATE_ROWS = 16


def _const_spec(shape):
    nd = len(shape)
    return pl.BlockSpec(shape, lambda *_: (0,) * nd, pipeline_mode=pl.Buffered(1))


def _params(sem, vmem=VMEM_LIMIT):
    return pltpu.CompilerParams(dimension_semantics=sem, vmem_limit_bytes=vmem)


def _rms(x, g):
    return x * lax.rsqrt(jnp.mean(x * x, axis=-1, keepdims=True) + EPS) * g


def _mm(a, w):
    return jnp.dot(a.astype(BF16), w, preferred_element_type=F32)


def _split2(x):
    hi = x.astype(BF16)
    lo = (x - hi.astype(F32)).astype(BF16)
    return hi, lo


def _split3(x):
    hi = x.astype(BF16)
    r1 = x - hi.astype(F32)
    mid = r1.astype(BF16)
    lo = (r1 - mid.astype(F32)).astype(BF16)
    return hi, mid, lo


def _dot_exact_rhs(x, w01):
    hi, lo = _split2(x)
    return (jnp.dot(hi, w01, preferred_element_type=F32)
            + jnp.dot(lo, w01, preferred_element_type=F32))


def _dot_exact_lhs(w01, x):
    hi, mid, lo = _split3(x)
    return (jnp.dot(w01, hi, preferred_element_type=F32)
            + jnp.dot(w01, mid, preferred_element_type=F32)
            + jnp.dot(w01, lo, preferred_element_type=F32))


def _rwkv_pre_kernel(has_vfirst, nct, *refs):
    (x_ref, shift0_ref, g0_ref, mu_ref, wrkv_ref, w0_ref, w1_ref, w2_ref, a0_ref, a1_ref, a2_ref,
     g1_ref, g2_ref, kk_ref, ka_ref, hsum_ref, hexp_ref, tri_ref, sel_ref) = refs[:19]
    refs = refs[19:]
    if has_vfirst:
        v0_ref, v1_ref, v2_ref, vfirst_ref = refs[:4]
        refs = refs[4:]
    rt_ref, at_ref, bt_ref, kt_ref, v_ref, g_ref, pt_ref, shift_ref, carry = refs

    tm = x_ref.shape[1]

    @pl.when(pl.program_id(1) == 0)
    def _():
        carry[...] = shift0_ref[0]

    h = _rms(x_ref[0], g0_ref[...])
    row = lax.broadcasted_iota(jnp.int32, h.shape, 0)
    prev = jnp.where(row == 0, carry[...], pltpu.roll(h, 1, axis=0))
    last = h[tm - 1:tm, :]
    carry[...] = last
    shift_ref[0] = last

    dx = prev - h
    xr, xw, xk, xv, xa, xg = (h + dx * mu_ref[c:c + 1, :] for c in range(6))
    r = _mm(xr, wrkv_ref[0])
    k = _mm(xk, wrkv_ref[1])
    v = _mm(xv, wrkv_ref[2])
    wl = w0_ref[...] + _mm(jnp.tanh(_mm(xw, w1_ref[...])), w2_ref[...])
    lw = (-math.exp(-0.5)) * jax.nn.sigmoid(wl)
    a = jax.nn.sigmoid(a0_ref[...] + _mm(_mm(xa, a1_ref[...]), a2_ref[...]))
    g = _mm(jax.nn.sigmoid(_mm(xg, g1_ref[...])), g2_ref[...])
    if has_vfirst:
        mix = jax.nn.sigmoid(v0_ref[...] + _mm(_mm(xv, v1_ref[...]), v2_ref[...]))
        v = v + (vfirst_ref[0] - v) * mix

    kkr = k * kk_ref[...]
    ssq = _dot_exact_rhs(kkr * kkr, hsum_ref[...])
    inv = lax.rsqrt(jnp.maximum(ssq, 1e-24))
    kk = kkr * _dot_exact_rhs(inv, hexp_ref[...])
    kmod = k * (1.0 + (a - 1.0) * ka_ref[...])

    b = _dot_exact_lhs(tri_ref[...], lw)
    eb = jnp.exp(b)
    enb = jnp.exp(-b)
    rt_ref[0] = r * eb
    at_ref[0] = -kk * jnp.exp(b - lw)
    bt_ref[0] = kk * a * enb
    kt_ref[0] = kmod * enb
    v_ref[0] = v
    g_ref[0] = g
    pt_ref[0, 0] = jnp.exp(_dot_exact_lhs(sel_ref[...], lw))[:nct]


def _rwkv_pre(x, shift0, p, j, v_first):
    B, T, D = x.shape
    L = min(CHUNK, T)
    tm = min(TOKEN_TILE, T)
    nct = tm // L
    has_vfirst = v_first is not None
    row = jnp.arange(tm)
    tri = ((row[:, None] >= row[None, :]) & (row[:, None] // L == row[None, :] // L)).astype(BF16)
    sel = (jnp.arange(GATE_ROWS)[:, None] == row[None, :] // L).astype(BF16)

    tile = pl.BlockSpec((1, tm, D), lambda b, t: (b, t, 0))
    consts = [p['g0'], p['mu'], p['w_rkv'], p['w0'], p['w1'], p['w2'], p['a0'], p['a1'], p['a2'],
              p['g1'], p['g2'], p['k_k'], p['k_a'], p['hsum'], p['hexp'], tri, sel]
    args = [x, shift0.reshape(B, 1, D)] + consts
    specs = [tile, pl.BlockSpec((1, 1, D), lambda b, t: (b, 0, 0))] + [_const_spec(c.shape) for c in consts]
    if has_vfirst:
        vc = [p['v0'], p['v1'], p['v2']]
        args += vc + [v_first]
        specs += [_const_spec(c.shape) for c in vc] + [tile]

    big = jax.ShapeDtypeStruct((B, T, D), F32)
    outs = pl.pallas_call(
        functools.partial(_rwkv_pre_kernel, has_vfirst, nct),
        grid=(B, T // tm),
        in_specs=specs,
        out_specs=[tile] * 6 + [pl.BlockSpec((1, 1, nct, D), lambda b, t: (b, t, 0, 0)),
                                pl.BlockSpec((1, 1, D), lambda b, t: (b, 0, 0))],
        out_shape=[big] * 6 + [jax.ShapeDtypeStruct((B, T // tm, nct, D), F32),
                               jax.ShapeDtypeStruct((B, 1, D), F32)],
        scratch_shapes=[pltpu.VMEM((1, D), F32)],
        compiler_params=_params(("parallel", "arbitrary")),
        name="rwkv_pre",
    )(*args)
    rt, at, bt, kt, v, g, pt, shift = outs
    return rt, at, bt, kt, v, g, pt.reshape(B, T // L, 1, D), shift.reshape(B, D)


def _rwkv_chunk_maps(L, items):
    N = RW_HEAD
    nt = (((1,), (1,)), ((), ()))
    tn = (((0,), (0,)), ((), ()))
    head0_f = lax.broadcasted_iota(jnp.int32, (1, 2 * N), 1) < N
    head0_t = lax.broadcasted_iota(jnp.int32, (1, 2 * L), 1) < L
    zero = jnp.zeros((), BF16)

    def split(z, first):
        return jnp.concatenate([jnp.where(first, z, zero), jnp.where(first, zero, z)], axis=0)

    row = lax.broadcasted_iota(jnp.int32, (L, 4 * L), 0)
    col = lax.broadcasted_iota(jnp.int32, (L, 4 * L), 1) % L
    strict = row > col
    incl = row >= col
    eye = jnp.where(lax.broadcasted_iota(jnp.int32, (L, 2 * L), 0)
                    == lax.broadcasted_iota(jnp.int32, (L, 2 * L), 1) % L, 1.0, 0.0)

    ats = [it[0].astype(BF16) for it in items]
    vbs = [it[4].astype(BF16) for it in items]
    v01s = [split(vb, head0_f) for vb in vbs]
    gs = [lax.dot_general(jnp.concatenate([at, it[1].astype(BF16)], axis=0),
                          jnp.concatenate([split(it[2].astype(BF16), head0_f),
                                           split(it[3].astype(BF16), head0_f)], axis=0),
                          nt, preferred_element_type=F32)
          for it, at in zip(items, ats)]
    tops = [jnp.where(strict, g[:L], 0.0) for g in gs]
    bots = [jnp.where(incl, g[L:], 0.0).astype(BF16) for g in gs]
    avs = [_mm(top[:, 2 * L:], v01) for top, v01 in zip(tops, v01s)]
    n_pows = [top[:, :2 * L].astype(BF16) for top in tops]
    ts = [eye + top[:, :2 * L] for top in tops]
    n_pows = [jnp.dot(n, split(n, head0_t), preferred_element_type=F32).astype(BF16) for n in n_pows]
    span = 2
    while True:
        span *= 2
        if span >= L:
            ts = [t + jnp.dot(n, split(t.astype(BF16), head0_t), preferred_element_type=F32)
                  for t, n in zip(ts, n_pows)]
            break
        res = [jnp.dot(n, jnp.concatenate([split(t.astype(BF16), head0_t), split(n, head0_t)], axis=1),
                       preferred_element_type=F32) for t, n in zip(ts, n_pows)]
        ts = [t + r[:, :2 * L] for t, r in zip(ts, res)]
        n_pows = [r[:, 2 * L:].astype(BF16) for r in res]

    wus = [_mm(t, jnp.concatenate([split(at, head0_f), split(av.astype(BF16), head0_f)], axis=1)).astype(BF16)
           for t, at, av in zip(ts, ats, avs)]
    zeros2 = jnp.zeros((2 * L, 2 * N), BF16)
    ros = [jnp.dot(bot, jnp.concatenate(
               [jnp.concatenate([split(wu[:, :2 * N], head0_f), split(wu[:, 2 * N:], head0_f)], axis=1),
                jnp.concatenate([zeros2, v01], axis=1)], axis=0), preferred_element_type=F32)
           for bot, wu, v01 in zip(bots, wus, v01s)]
    zs = [jnp.concatenate([wu, jnp.concatenate([zeros2[:L], vb], axis=1)], axis=0)
          for wu, vb in zip(wus, vbs)]
    mcs = [lax.dot_general(z, jnp.concatenate([it[2] * it[5], it[3] * it[5]], axis=0).astype(BF16), tn,
                           preferred_element_type=F32)
           for it, z in zip(items, zs)]
    return [(it[1] + ro[:, :2 * N], ro[:, 2 * N:],
             jnp.where(head0_f, mc[:N], mc[N:2 * N]), jnp.where(head0_f, mc[2 * N:3 * N], mc[3 * N:]))
            for it, ro, mc in zip(items, ros, mcs)]


def _rwkv_maps_kernel(L, rt_ref, at_ref, bt_ref, kt_ref, v_ref, pt_ref, rk_ref,
                      rhat_ref, ohat_ref, bonus_ref, m_ref, c_ref):
    N = RW_HEAD
    nck = rt_ref.shape[1] // L
    where = [(slice(c * L, (c + 1) * L), slice(p * LANES, (p + 1) * LANES), c)
             for c in range(nck) for p in range(rt_ref.shape[2] // LANES)]
    items = [(at_ref[0, rows, sl], rt_ref[0, rows, sl], bt_ref[0, rows, sl], kt_ref[0, rows, sl],
              v_ref[0, rows, sl], pt_ref[0, c, :, sl]) for rows, sl, c in where]
    maps = _rwkv_chunk_maps(L, items)
    head0 = lax.broadcasted_iota(jnp.int32, (1, LANES), 1) < N
    for (rows, sl, c), it, (rhat, ohat, m, cc) in zip(where, items, maps):
        _, rt, _, kt, v, _ = it
        rk = rt * kt * rk_ref[:, sl]
        total = jnp.sum(rk, axis=-1, keepdims=True)
        part = jnp.sum(jnp.where(head0, rk, 0.0), axis=-1, keepdims=True)
        rhat_ref[0, rows, sl] = rhat.astype(BF16)
        ohat_ref[0, rows, sl] = ohat
        bonus_ref[0, rows, sl] = jnp.where(head0, part, total - part) * v
        m_ref[0, c, :, sl] = m.astype(BF16)
        c_ref[0, c, :, sl] = cc


def _rwkv_maps(rt, at, bt, kt, v, pt, r_k):
    B, T, D = rt.shape
    L = min(CHUNK, T)
    ts = min(MAPS_TILE, T)
    nck = ts // L
    N = RW_HEAD
    tile = pl.BlockSpec((1, ts, MAPS_LANES), lambda b, h, c: (b, c, h))
    sq = pl.BlockSpec((1, nck, N, MAPS_LANES), lambda b, h, c: (b, c, 0, h))
    big = lambda dt: jax.ShapeDtypeStruct((B, T, D), dt)
    return pl.pallas_call(
        functools.partial(_rwkv_maps_kernel, L),
        grid=(B, D // MAPS_LANES, T // ts),
        in_specs=[tile] * 5 + [pl.BlockSpec((1, nck, 1, MAPS_LANES), lambda b, h, c: (b, c, 0, h)),
                               pl.BlockSpec((1, MAPS_LANES), lambda b, h, c: (0, h))],
        out_specs=[tile, tile, tile, sq, sq],
        out_shape=[big(BF16), big(F32), big(F32),
                   jax.ShapeDtypeStruct((B, T // L, N, D), BF16), jax.ShapeDtypeStruct((B, T // L, N, D), F32)],
        compiler_params=_params(("parallel", "parallel", "parallel")),
        name="rwkv_maps",
    )(rt, at, bt, kt, v, pt, r_k)


def _rwkv_state_kernel(L, rhat_ref, ohat_ref, bonus_ref, m_ref, c_ref, pt_ref, s0_ref, lnw_ref, lnb_ref,
                       o_ref, sT_ref, state):
    N = RW_HEAD
    nck = rhat_ref.shape[1] // L
    npairs = rhat_ref.shape[2] // LANES

    @pl.when(pl.program_id(1) == 0)
    def _():
        state[...] = s0_ref[0]

    nt = (((1,), (1,)), ((), ()))
    lane = lax.broadcasted_iota(jnp.int32, (1, LANES), 1)
    first = lane < N
    zero = jnp.zeros((), BF16)

    def block_diag(a):
        return jnp.concatenate([jnp.where(first, a, zero), jnp.where(first, zero, a)], axis=0)

    def head_sum(a):
        total = jnp.sum(a, axis=-1, keepdims=True)
        part = jnp.sum(jnp.where(first, a, 0.0), axis=-1, keepdims=True)
        return jnp.where(first, part, total - part)

    s = [state[:, p * LANES:(p + 1) * LANES] for p in range(npairs)]
    for c in range(nck):
        rows = slice(c * L, (c + 1) * L)
        os = []
        for p in range(npairs):
            pl_ = slice(p * LANES, (p + 1) * LANES)
            sb = s[p].astype(BF16)
            os.append(lax.dot_general(rhat_ref[0, rows, pl_], block_diag(sb), nt, preferred_element_type=F32)
                      + ohat_ref[0, rows, pl_])
            s[p] = (s[p] * pt_ref[0, c, :, pl_]
                    + jnp.dot(sb, block_diag(m_ref[0, c, :, pl_]), preferred_element_type=F32)
                    + c_ref[0, c, :, pl_])
        ds = [o - m for o, m in zip(os, [head_sum(o) * (1.0 / N) for o in os])]
        variances = [head_sum(d * d) * (1.0 / N) for d in ds]
        for p, (d, var) in enumerate(zip(ds, variances)):
            pl_ = slice(p * LANES, (p + 1) * LANES)
            o_ref[0, rows, pl_] = (d * lax.rsqrt(var + GN_EPS) * lnw_ref[:, pl_] + lnb_ref[:, pl_]
                                   + bonus_ref[0, rows, pl_])
    for p in range(npairs):
        state[:, p * LANES:(p + 1) * LANES] = s[p]

    @pl.when(pl.program_id(1) == pl.num_programs(1) - 1)
    def _():
        sT_ref[0] = state[...]


def _rwkv_state(rhat, ohat, bonus, m, c, pt, s0, p):
    B, T, D = ohat.shape
    L = min(CHUNK, T)
    ts = min(STATE_TILE, T)
    nck = ts // L
    N = RW_HEAD
    H = D // N
    tile = pl.BlockSpec((1, ts, D), lambda b, t: (b, t, 0))
    sq = pl.BlockSpec((1, nck, N, D), lambda b, t: (b, t, 0, 0))
    st = pl.BlockSpec((1, N, D), lambda b, t: (b, 0, 0))
    s0_vk = jnp.transpose(s0, (0, 2, 1, 3)).reshape(B, N, D)
    o, sT = pl.pallas_call(
        functools.partial(_rwkv_state_kernel, L),
        grid=(B, T // ts),
        in_specs=[tile, tile, tile, sq, sq, pl.BlockSpec((1, nck, 1, D), lambda b, t: (b, t, 0, 0)), st,
                  _const_spec((1, D)), _const_spec((1, D))],
        out_specs=[tile, st],
        out_shape=[jax.ShapeDtypeStruct((B, T, D), F32), jax.ShapeDtypeStruct((B, N, D), F32)],
        scratch_shapes=[pltpu.VMEM((N, D), F32)],
        compiler_params=_params(("parallel", "arbitrary")),
        name="rwkv_state",
    )(rhat, ohat, bonus, m, c, pt, s0_vk, p['ln_w'], p['ln_b'])
    return o, jnp.transpose(sT.reshape(B, N, H, N), (0, 2, 1, 3))


def _rwkv_scan(rt, at, bt, kt, v, pt, s0, p):
    rhat, ohat, bonus, m, c = _rwkv_maps(rt, at, bt, kt, v, pt, p['r_k'])
    return _rwkv_state(rhat, ohat, bonus, m, c, pt, s0, p)


def _post_kernel(x_ref, a_ref, b_ref, wo_ref, gmix_ref, gpre_ref, wgu_ref, wd_ref, gpost_ref, o_ref):
    f = wd_ref.shape[0]
    x = x_ref[...] + _rms(_mm(a_ref[...] * b_ref[...], wo_ref[...]), gmix_ref[...])
    gu = _mm(_rms(x, gpre_ref[...]), wgu_ref[...])
    gate = gu[:, :f]
    act = gate * jax.nn.sigmoid(gate) * gu[:, f:]
    o_ref[...] = x + _rms(_mm(act, wd_ref[...]), gpost_ref[...])


def _post(x, a, b, wo, lp):
    R, D = x.shape
    tm = min(TOKEN_TILE, R)
    tile = pl.BlockSpec((tm, D), lambda t: (t, 0))
    consts = (wo, lp['g_mix'], lp['g_ffn_pre'], lp['ffn_w_gu'], lp['ffn_w_down'], lp['g_ffn_post'])
    return pl.pallas_call(
        _post_kernel,
        grid=(R // tm,),
        in_specs=[tile, tile, tile] + [_const_spec(c.shape) for c in consts],
        out_specs=tile,
        out_shape=jax.ShapeDtypeStruct((R, D), F32),
        compiler_params=_params(("parallel",)),
        name="post",
    )(x, a, b, *consts)


def _mlstm_pre_kernel(L, nq, x_ref, g0_ref, win_ref, wgc_ref, wgr_ref, bc_ref, br_ref,
                      q_ref, k_ref, v_ref, og_ref, gc_ref, gr_ref):
    h = _rms(x_ref[...], g0_ref[...]).astype(BF16)
    z = jnp.dot(h, win_ref[...], preferred_element_type=F32)
    nv = v_ref.shape[1]
    dqk = nq // ML_HEADS
    q_ref[...] = z[:, :nq] * (dqk ** -0.5)
    k_ref[...] = z[:, nq:2 * nq]
    v_ref[...] = z[:, 2 * nq:2 * nq + nv]
    og_ref[...] = jax.nn.sigmoid(z[:, 2 * nq + nv:])

    def cap(t):
        return GATE_CAP * jnp.tanh(t / GATE_CAP)

    gc = cap(jnp.dot(h, wgc_ref[...], preferred_element_type=F32) + bc_ref[...])
    lane = lax.broadcasted_iota(jnp.int32, gc.shape, 1)
    gc_ref[...] = jnp.where(lane < ML_HEADS, gc, jax.nn.log_sigmoid(gc))
    gr = cap(lax.dot_general(wgr_ref[...], h, (((1,), (1,)), ((), ())), preferred_element_type=F32)
             + br_ref[...])
    rowi = lax.broadcasted_iota(jnp.int32, gr.shape, 0)
    gr = jnp.where(rowi < ML_HEADS, gr, jax.nn.log_sigmoid(gr))
    for c in range(gr_ref.shape[0]):
        gr_ref[c] = gr[:, c * L:(c + 1) * L]


def _mlstm_pre(x, L, p):
    R, D = x.shape
    tm = min(TOKEN_TILE, R)
    nq = ML_HEADS * (D // (2 * ML_HEADS))
    nv = D
    consts = [p['g0'], p['w_in'], p['w_gc'], p['w_gr'], p['b_c'], p['b_r']]
    row = lambda w: pl.BlockSpec((tm, w), lambda t: (t, 0))
    return pl.pallas_call(
        functools.partial(_mlstm_pre_kernel, L, nq),
        grid=(R // tm,),
        in_specs=[row(D)] + [_const_spec(c.shape) for c in consts],
        out_specs=[row(nq), row(nq), row(nv), row(nv), row(LANES),
                   pl.BlockSpec((tm // L, GATE_ROWS, L), lambda t: (t, 0, 0))],
        out_shape=[jax.ShapeDtypeStruct((R, nq), F32), jax.ShapeDtypeStruct((R, nq), F32),
                   jax.ShapeDtypeStruct((R, nv), F32), jax.ShapeDtypeStruct((R, nv), F32),
                   jax.ShapeDtypeStruct((R, LANES), F32),
                   jax.ShapeDtypeStruct((R // L, GATE_ROWS, L), F32)],
        compiler_params=_params(("parallel",)),
        name="mlstm_pre",
    )(x, *consts)


def _mlstm_scan_kernel(L, q_ref, k_ref, v_ref, gc_ref, gr_ref, c0_ref, n0_ref, m0_ref, hnw_ref,
                       h_ref, cT_ref, nT_ref, mT_ref, c_st, n_st, m_st):
    H = ML_HEADS
    nck = q_ref.shape[0] // L
    dqk = q_ref.shape[1] // H
    dv = v_ref.shape[1] // H

    @pl.when(pl.program_id(1) == 0)
    def _():
        c_st[...] = c0_ref[0]
        n_st[...] = n0_ref[0]
        m_st[...] = m0_ref[0]

    row = lax.broadcasted_iota(jnp.int32, (L, L), 0)
    col = lax.broadcasted_iota(jnp.int32, (L, L), 1)
    causal = row >= col
    tril = causal.astype(BF16)
    triu = (row <= col).astype(BF16)
    nt = (((1,), (1,)), ((), ()))
    tn = (((0,), (0,)), ((), ()))

    gcs = [gc_ref[c * L:(c + 1) * L, :] for c in range(nck)]
    grs = [gr_ref[c] for c in range(nck)]
    bcs = [_dot_exact_lhs(tril, gc) for gc in gcs]
    brs = []
    for gr in grs:
        hi, mid, lo = _split3(gr)
        brs.append(jnp.dot(hi, triu, preferred_element_type=F32) + jnp.dot(mid, triu, preferred_element_type=F32)
                   + jnp.dot(lo, triu, preferred_element_type=F32))

    where = [(c, hd) for c in range(nck) for hd in range(H)]
    rows_of = [slice(c * L, (c + 1) * L) for c, _ in where]
    qs = [q_ref[rows, hd * dqk:(hd + 1) * dqk] for rows, (_, hd) in zip(rows_of, where)]
    ks = [k_ref[rows, hd * dqk:(hd + 1) * dqk] for rows, (_, hd) in zip(rows_of, where)]
    vbs = [v_ref[rows, hd * dv:(hd + 1) * dv].astype(BF16) for rows, (_, hd) in zip(rows_of, where)]
    li_cs = [gcs[c][:, hd:hd + 1] for c, hd in where]
    b_cs = [bcs[c][:, H + hd:H + hd + 1] for c, hd in where]
    dmats = [jnp.where(causal, b_c - brs[c][H + hd:H + hd + 1, :] + grs[c][hd:hd + 1, :], -jnp.inf)
             for (c, hd), b_c in zip(where, b_cs)]
    intras = [jnp.max(dmat, axis=-1, keepdims=True) for dmat in dmats]
    gs = [intra - b_c for intra, b_c in zip(intras, b_cs)]
    g_lasts = [g[L - 1:L, :] for g in gs]
    b_lasts = [b_c[L - 1:L, :] for b_c in b_cs]
    es = [jnp.exp(dmat - intra) for dmat, intra in zip(dmats, intras)]
    kws = [k * jnp.exp(li_c - b_c - g_last) for k, li_c, b_c, g_last in zip(ks, li_cs, b_cs, g_lasts)]
    qbs = [q.astype(BF16) for q in qs]
    qks = [lax.dot_general(qb, k.astype(BF16), nt, preferred_element_type=F32) * e
           for qb, k, e in zip(qbs, ks, es)]
    num_i = [_mm(qk, vb) for qk, vb in zip(qks, vbs)]
    kvs = [lax.dot_general(kw.astype(BF16), vb, tn, preferred_element_type=F32) for kw, vb in zip(kws, vbs)]
    ksums = [jnp.sum(kw, axis=0, keepdims=True) for kw in kws]
    qk_sums = [jnp.sum(qk, axis=-1, keepdims=True) for qk in qks]

    c_cur = [c_st[hd] for hd in range(H)]
    n_cur = [n_st[hd] for hd in range(H)]
    m_cur = [m_st[hd] for hd in range(H)]
    c_prevs, n_prevs, m_prevs = [], [], []
    for (c, hd), g_last, b_last, kv, ksum in zip(where, g_lasts, b_lasts, kvs, ksums):
        c_prevs.append(c_cur[hd].astype(BF16))
        n_prevs.append(n_cur[hd])
        m_prevs.append(m_cur[hd])
        mx = jnp.maximum(m_cur[hd], g_last)
        e_old = jnp.exp(m_cur[hd] - mx)
        e_new = jnp.exp(g_last - mx)
        c_cur[hd] = e_old * c_cur[hd] + e_new * kv
        n_cur[hd] = e_old * n_cur[hd] + e_new * ksum
        m_cur[hd] = b_last + mx
    for hd in range(H):
        c_st[hd] = c_cur[hd]
        n_st[hd] = n_cur[hd]
        m_st[hd] = m_cur[hd]

    qcs = [_mm(qb, cp) for qb, cp in zip(qbs, c_prevs)]
    qn_sums = [jnp.sum(q * n_prev, axis=-1, keepdims=True) for q, n_prev in zip(qs, n_prevs)]
    mxs = [jnp.maximum(m_prev, g) for g, m_prev in zip(gs, m_prevs)]
    f_news = [jnp.exp(g - mx) for g, mx in zip(gs, mxs)]
    f_olds = [jnp.exp(m_prev - mx) for mx, m_prev in zip(mxs, m_prevs)]
    floors = [jnp.exp(-(b_c + mx)) for b_c, mx in zip(b_cs, mxs)]
    dens = [f_new * qk_sum + f_old * qn_sum
            for f_new, f_old, qk_sum, qn_sum in zip(f_news, f_olds, qk_sums, qn_sums)]
    hs = [(f_new * ni + f_old * qc) / jnp.maximum(jnp.abs(den), floor)
          for f_new, f_old, ni, qc, den, floor in zip(f_news, f_olds, num_i, qcs, dens, floors)]
    h_ms = [jnp.mean(h * h, axis=-1, keepdims=True) for h in hs]
    for rows, (_, hd), h, h_m in zip(rows_of, where, hs, h_ms):
        h_ref[rows, hd * dv:(hd + 1) * dv] = h * lax.rsqrt(h_m + EPS) * hnw_ref[:, hd * dv:(hd + 1) * dv]

    @pl.when(pl.program_id(1) == pl.num_programs(1) - 1)
    def _():
        cT_ref[0] = c_st[...]
        nT_ref[0] = n_st[...]
        mT_ref[0] = m_st[...]


def _mlstm_scan(q, k, v, gc, gr, c0, n0, m0, hn_w, B, T):
    L = min(CHUNK, T)
    ts = min(MLSTM_TILE, T)
    nc = T // ts
    H = ML_HEADS
    nq, nv = q.shape[1], v.shape[1]
    dqk, dv = nq // H, nv // H
    row = lambda w: pl.BlockSpec((ts, w), lambda b, c: (b * nc + c, 0))
    st = lambda *s: pl.BlockSpec((1,) + s, lambda b, c: (b,) + (0,) * len(s))
    return pl.pallas_call(
        functools.partial(_mlstm_scan_kernel, L),
        grid=(B, nc),
        in_specs=[row(nq), row(nq), row(nv), row(LANES),
                  pl.BlockSpec((ts // L, GATE_ROWS, L), lambda b, c: (b * nc + c, 0, 0)),
                  st(H, dqk, dv), st(H, 1, dqk), st(H, 1, 1), _const_spec(hn_w.shape)],
        out_specs=[row(nv), st(H, dqk, dv), st(H, 1, dqk), st(H, 1, 1)],
        out_shape=[jax.ShapeDtypeStruct((B * T, nv), F32), jax.ShapeDtypeStruct((B, H, dqk, dv), F32),
                   jax.ShapeDtypeStruct((B, H, 1, dqk), F32), jax.ShapeDtypeStruct((B, H, 1, 1), F32)],
        scratch_shapes=[pltpu.VMEM((H, dqk, dv), F32), pltpu.VMEM((H, 1, dqk), F32),
                        pltpu.VMEM((H, 1, 1), F32)],
        compiler_params=_params(("parallel", "arbitrary")),
        name="mlstm_scan",
    )(q, k, v, gc, gr, c0, n0.reshape(B, H, 1, dqk), m0.reshape(B, H, 1, 1), hn_w)


def _prep_weights(w):
    D = w['norm_g'].shape[-1]
    H = D // RW_HEAD
    row = lambda a: a.reshape(1, -1).astype(F32)
    head_of_lane = jnp.arange(D) // RW_HEAD
    hsum = (head_of_lane[:, None] == jnp.arange(LANES)[None, :]).astype(BF16)
    hexp = hsum.T
    layers = []
    for i in range(DEPTH):
        j = i // N_MIXERS
        ng = w['norm_g'][i]
        lp = {'g0': row(ng[0]), 'g_mix': row(ng[1]), 'g_ffn_pre': row(ng[2]), 'g_ffn_post': row(ng[3]),
              'ffn_w_gu': w['ffn_w_gu'][i].astype(BF16), 'ffn_w_down': w['ffn_w_down'][i].astype(BF16)}
        if i % N_MIXERS == 0:
            lp.update({
                'mu': jnp.pad(w['rw_mu'][j], ((0, 2), (0, 0))),
                'w_rkv': w['rw_w_rkv'][j].astype(BF16),
                'w0': row(w['rw_w0'][j]), 'w1': w['rw_w1'][j].astype(BF16), 'w2': w['rw_w2'][j].astype(BF16),
                'a0': row(w['rw_a0'][j]), 'a1': w['rw_a1'][j].astype(BF16), 'a2': w['rw_a2'][j].astype(BF16),
                'g1': w['rw_g1'][j].astype(BF16), 'g2': w['rw_g2'][j].astype(BF16),
                'k_k': row(w['rw_k_k'][j]), 'k_a': row(w['rw_k_a'][j]), 'r_k': row(w['rw_r_k'][j]),
                'ln_w': row(w['rw_ln_w'][j]), 'ln_b': row(w['rw_ln_b'][j]),
                'w_o': w['rw_w_o'][j].astype(BF16), 'hsum': hsum, 'hexp': hexp,
            })
            if j > 0:
                lp.update({'v0': row(w['rw_v0'][j - 1]), 'v1': w['rw_v1'][j - 1].astype(BF16),
                           'v2': w['rw_v2'][j - 1].astype(BF16)})
        else:
            w_in = w['ml_w_in'][j]
            n_main = w_in.shape[1] - 2 * ML_HEADS
            w_g = w_in[:, n_main:]
            bias = w['ml_b_gates'][j].astype(F32)
            lp.update({
                'w_in': w_in[:, :n_main].astype(BF16),
                'w_gc': jnp.pad(w_g, ((0, 0), (0, LANES - 2 * ML_HEADS))).astype(BF16),
                'w_gr': jnp.pad(w_g.T, ((0, GATE_ROWS - 2 * ML_HEADS), (0, 0))).astype(BF16),
                'b_c': jnp.pad(bias, (0, LANES - 2 * ML_HEADS)).reshape(1, LANES),
                'b_r': jnp.pad(bias, (0, GATE_ROWS - 2 * ML_HEADS)).reshape(GATE_ROWS, 1),
                'hn_w': row(w['ml_hn_w'][j]), 'w_out': w['ml_w_out'][j].astype(BF16),
            })
        layers.append(lp)
    return layers


def _trunk(x, rw_shift, rw_S, ml_C, ml_n, ml_m, layers):
    B, T, D = x.shape
    L = min(CHUNK, T)
    flat = lambda a: a.reshape(B * T, a.shape[-1])
    v_first = None
    shifts, Ss, Cs, ns, ms = [], [], [], [], []
    xf = flat(x)
    for i, lp in enumerate(layers):
        j = i // N_MIXERS
        if i % N_MIXERS == 0:
            rt, at, bt, kt, v, g, pt, sh = _rwkv_pre(xf.reshape(B, T, D), rw_shift[j], lp, j, v_first)
            if j == 0:
                v_first = v
            o, S = _rwkv_scan(rt, at, bt, kt, v, pt, rw_S[j], lp)
            shifts.append(sh)
            Ss.append(S)
            xf = _post(xf, flat(o), flat(g), lp['w_o'], lp)
        else:
            q, k, v, og, gc, gr = _mlstm_pre(xf, L, lp)
            hn, C, n, m = _mlstm_scan(q, k, v, gc, gr, ml_C[j], ml_n[j], ml_m[j], lp['hn_w'], B, T)
            Cs.append(C)
            ns.append(n.reshape(B, ML_HEADS, -1))
            ms.append(m.reshape(B, ML_HEADS))
            xf = _post(xf, og, hn, lp['w_out'], lp)
    return xf.reshape(B, T, D), jnp.stack(shifts), jnp.stack(Ss), jnp.stack(Cs), jnp.stack(ns), jnp.stack(ms)


def kernel(x_prompt, x_sample, state_rwkv_shift, state_rwkv_S, state_mlstm_C, state_mlstm_n, state_mlstm_m,
           norm_g, rw_mu, rw_w_rkv, rw_w0, rw_w1, rw_w2, rw_a0, rw_a1, rw_a2, rw_v0, rw_v1, rw_v2,
           rw_g1, rw_g2, rw_k_k, rw_k_a, rw_r_k, rw_ln_w, rw_ln_b, rw_w_o,
           ml_w_in, ml_b_gates, ml_hn_w, ml_w_out, ffn_w_gu, ffn_w_down):
    w = {
        'norm_g': norm_g, 'rw_mu': rw_mu, 'rw_w_rkv': rw_w_rkv, 'rw_w0': rw_w0, 'rw_w1': rw_w1,
        'rw_w2': rw_w2, 'rw_a0': rw_a0, 'rw_a1': rw_a1, 'rw_a2': rw_a2, 'rw_v0': rw_v0, 'rw_v1': rw_v1,
        'rw_v2': rw_v2, 'rw_g1': rw_g1, 'rw_g2': rw_g2, 'rw_k_k': rw_k_k, 'rw_k_a': rw_k_a,
        'rw_r_k': rw_r_k, 'rw_ln_w': rw_ln_w, 'rw_ln_b': rw_ln_b, 'rw_w_o': rw_w_o,
        'ml_w_in': ml_w_in, 'ml_b_gates': ml_b_gates, 'ml_hn_w': ml_hn_w, 'ml_w_out': ml_w_out,
        'ffn_w_gu': ffn_w_gu, 'ffn_w_down': ffn_w_down,
    }
    layers = _prep_weights(w)
    Bp = x_prompt.shape[0]
    zeros = lambda a: jnp.zeros((a.shape[0], Bp) + a.shape[2:], F32)
    out_p = _trunk(x_prompt, zeros(state_rwkv_shift), zeros(state_rwkv_S), zeros(state_mlstm_C),
                   zeros(state_mlstm_n), zeros(state_mlstm_m), layers)
    out_s = _trunk(x_sample, state_rwkv_shift, state_rwkv_S, state_mlstm_C, state_mlstm_n, state_mlstm_m,
                   layers)
    return (out_p[0], out_s[0]) + tuple(out_p[1:]) + tuple(out_s[1:])
```

```python
import functools
import math

import jax
import jax.numpy as jnp
from jax import lax
from jax.experimental import pallas as pl
from jax.experimental.pallas import tpu as pltpu

F32 = jnp.float32
BF16 = jnp.bfloat16

DEPTH = 4
N_MIXERS = 2
CHUNK = 64
RW_HEAD = 64
ML_HEADS = 4
GN_EPS = 64e-5
GATE_CAP = 15.0
EPS = 1e-6

LANES = 128
VMEM_LIMIT = 56 * 1024 * 1024
TOKEN_TILE = 512
RWKV_PRE_TILE = 256
MAPS_TILE = 512
MAPS_LANES = 256
STATE_TILE = 256
MLSTM_TILE = 256
GATE_ROWS = 16


def _const_spec(shape):
    nd = len(shape)
    return pl.BlockSpec(shape, lambda *_: (0,) * nd, pipeline_mode=pl.Buffered(1))


def _params(sem, vmem=VMEM_LIMIT):
    return pltpu.CompilerParams(dimension_semantics=sem, vmem_limit_bytes=vmem)


def _rms(x, g):
    return x * lax.rsqrt(jnp.mean(x * x, axis=-1, keepdims=True) + EPS) * g


def _mm(a, w):
    return jnp.dot(a.astype(BF16), w, preferred_element_type=F32)


def _split2(x):
    hi = x.astype(BF16)
    lo = (x - hi.astype(F32)).astype(BF16)
    return hi, lo


def _split3(x):
    hi = x.astype(BF16)
    r1 = x - hi.astype(F32)
    mid = r1.astype(BF16)
    lo = (r1 - mid.astype(F32)).astype(BF16)
    return hi, mid, lo


def _dot_exact_rhs(x, w01):
    hi, lo = _split2(x)
    return (jnp.dot(hi, w01, preferred_element_type=F32)
            + jnp.dot(lo, w01, preferred_element_type=F32))


def _dot_exact_lhs(w01, x):
    hi, mid, lo = _split3(x)
    return (jnp.dot(w01, hi, preferred_element_type=F32)
            + jnp.dot(w01, mid, preferred_element_type=F32)
            + jnp.dot(w01, lo, preferred_element_type=F32))


def _rwkv_pre_kernel(has_vfirst, nct, *refs):
    (x_ref, shift0_ref, g0_ref, mu_ref, wrkv_ref, w0_ref, w1_ref, w2_ref, a0_ref, a1_ref, a2_ref,
     g1_ref, g2_ref, kk_ref, ka_ref, hsum_ref, hexp_ref, tri_ref, sel_ref) = refs[:19]
    refs = refs[19:]
    if has_vfirst:
        v0_ref, v1_ref, v2_ref, vfirst_ref = refs[:4]
        refs = refs[4:]
    rt_ref, at_ref, bt_ref, kt_ref, v_ref, g_ref, pt_ref, shift_ref, carry = refs

    tm = x_ref.shape[1]

    @pl.when(pl.program_id(1) == 0)
    def _():
        carry[...] = shift0_ref[0]

    h = _rms(x_ref[0], g0_ref[...])
    row = lax.broadcasted_iota(jnp.int32, h.shape, 0)
    prev = jnp.where(row == 0, carry[...], pltpu.roll(h, 1, axis=0))
    last = h[tm - 1:tm, :]
    carry[...] = last
    shift_ref[0] = last

    dx = prev - h
    xr, xw, xk, xv, xa, xg = (h + dx * mu_ref[c:c + 1, :] for c in range(6))
    r = _mm(xr, wrkv_ref[0])
    k = _mm(xk, wrkv_ref[1])
    v = _mm(xv, wrkv_ref[2])
    wl = w0_ref[...] + _mm(jnp.tanh(_mm(xw, w1_ref[...])), w2_ref[...])
    lw = (-math.exp(-0.5)) * jax.nn.sigmoid(wl)
    a = jax.nn.sigmoid(a0_ref[...] + _mm(_mm(xa, a1_ref[...]), a2_ref[...]))
    g = _mm(jax.nn.sigmoid(_mm(xg, g1_ref[...])), g2_ref[...])
    if has_vfirst:
        mix = jax.nn.sigmoid(v0_ref[...] + _mm(_mm(xv, v1_ref[...]), v2_ref[...]))
        v = v + (vfirst_ref[0] - v) * mix

    kkr = k * kk_ref[...]
    ssq = _mm(kkr * kkr, hsum_ref[...])
    inv = lax.rsqrt(jnp.maximum(ssq, 1e-24))
    kk = kkr * _dot_exact_rhs(inv, hexp_ref[...])
    kmod = k * (1.0 + (a - 1.0) * ka_ref[...])

    lw_hi, lw_lo = _split2(lw)

    def time_sum(w01):
        return jnp.dot(w01, lw_hi, preferred_element_type=F32) + jnp.dot(w01, lw_lo, preferred_element_type=F32)

    b = time_sum(tri_ref[...])
    eb = jnp.exp(b)
    enb = jnp.exp(-b)
    rt_ref[0] = r * eb
    at_ref[0] = -kk * jnp.exp(b - lw)
    bt_ref[0] = kk * a * enb
    kt_ref[0] = kmod * enb
    v_ref[0] = v
    g_ref[0] = g
    pt_ref[0, 0] = jnp.exp(time_sum(sel_ref[...]))[:nct]


def _rwkv_pre(x, shift0, p, j, v_first):
    B, T, D = x.shape
    L = min(CHUNK, T)
    tm = min(RWKV_PRE_TILE, T)
    nct = tm // L
    has_vfirst = v_first is not None
    row = jnp.arange(tm)
    tri = ((row[:, None] >= row[None, :]) & (row[:, None] // L == row[None, :] // L)).astype(BF16)
    sel = (jnp.arange(GATE_ROWS)[:, None] == row[None, :] // L).astype(BF16)

    tile = pl.BlockSpec((1, tm, D), lambda b, t: (b, t, 0))
    consts = [p['g0'], p['mu'], p['w_rkv'], p['w0'], p['w1'], p['w2'], p['a0'], p['a1'], p['a2'],
              p['g1'], p['g2'], p['k_k'], p['k_a'], p['hsum'], p['hexp'], tri, sel]
    args = [x, shift0.reshape(B, 1, D)] + consts
    specs = [tile, pl.BlockSpec((1, 1, D), lambda b, t: (b, 0, 0))] + [_const_spec(c.shape) for c in consts]
    if has_vfirst:
        vc = [p['v0'], p['v1'], p['v2']]
        args += vc + [v_first]
        specs += [_const_spec(c.shape) for c in vc] + [tile]

    big = jax.ShapeDtypeStruct((B, T, D), F32)
    outs = pl.pallas_call(
        functools.partial(_rwkv_pre_kernel, has_vfirst, nct),
        grid=(B, T // tm),
        in_specs=specs,
        out_specs=[tile] * 6 + [pl.BlockSpec((1, 1, nct, D), lambda b, t: (b, t, 0, 0)),
                                pl.BlockSpec((1, 1, D), lambda b, t: (b, 0, 0))],
        out_shape=[big] * 6 + [jax.ShapeDtypeStruct((B, T // tm, nct, D), F32),
                               jax.ShapeDtypeStruct((B, 1, D), F32)],
        scratch_shapes=[pltpu.VMEM((1, D), F32)],
        compiler_params=_params(("parallel", "arbitrary")),
        name="rwkv_pre",
    )(*args)
    rt, at, bt, kt, v, g, pt, shift = outs
    return rt, at, bt, kt, v, g, pt.reshape(B, T // L, 1, D), shift.reshape(B, D)


def _rwkv_chunk_maps(L, items):
    N = RW_HEAD
    nt = (((1,), (1,)), ((), ()))
    tn = (((0,), (0,)), ((), ()))
    head0_f = lax.broadcasted_iota(jnp.int32, (1, 2 * N), 1) < N
    head0_t = lax.broadcasted_iota(jnp.int32, (1, 2 * L), 1) < L
    zero = jnp.zeros((), BF16)

    def split(z, first):
        return jnp.concatenate([jnp.where(first, z, zero), jnp.where(first, zero, z)], axis=0)

    row = lax.broadcasted_iota(jnp.int32, (L, 4 * L), 0)
    col = lax.broadcasted_iota(jnp.int32, (L, 4 * L), 1) % L
    strict = row > col
    incl = row >= col
    eye = jnp.where(lax.broadcasted_iota(jnp.int32, (L, 2 * L), 0)
                    == lax.broadcasted_iota(jnp.int32, (L, 2 * L), 1) % L, 1.0, 0.0)

    ats = [it[0].astype(BF16) for it in items]
    vbs = [it[4].astype(BF16) for it in items]
    v01s = [split(vb, head0_f) for vb in vbs]
    gs = [lax.dot_general(jnp.concatenate([at, it[1].astype(BF16)], axis=0),
                          jnp.concatenate([split(it[2].astype(BF16), head0_f),
                                           split(it[3].astype(BF16), head0_f)], axis=0),
                          nt, preferred_element_type=F32)
          for it, at in zip(items, ats)]
    tops = [jnp.where(strict, g[:L], 0.0) for g in gs]
    bots = [jnp.where(incl, g[L:], 0.0).astype(BF16) for g in gs]
    avs = [_mm(top[:, 2 * L:], v01) for top, v01 in zip(tops, v01s)]
    n_pows = [top[:, :2 * L].astype(BF16) for top in tops]
    ts = [eye + top[:, :2 * L] for top in tops]
    n_pows = [jnp.dot(n, split(n, head0_t), preferred_element_type=F32).astype(BF16) for n in n_pows]
    span = 2
    while True:
        span *= 2
        if span >= L:
            ts = [t + jnp.dot(n, split(t.astype(BF16), head0_t), preferred_element_type=F32)
                  for t, n in zip(ts, n_pows)]
            break
        res = [jnp.dot(n, jnp.concatenate([split(t.astype(BF16), head0_t), split(n, head0_t)], axis=1),
                       preferred_element_type=F32) for t, n in zip(ts, n_pows)]
        ts = [t + r[:, :2 * L] for t, r in zip(ts, res)]
        n_pows = [r[:, 2 * L:].astype(BF16) for r in res]

    wus = [_mm(t, jnp.concatenate([split(at, head0_f), split(av.astype(BF16), head0_f)], axis=1)).astype(BF16)
           for t, at, av in zip(ts, ats, avs)]
    zeros2 = jnp.zeros((2 * L, 2 * N), BF16)
    ros = [jnp.dot(bot, jnp.concatenate(
               [jnp.concatenate([split(wu[:, :2 * N], head0_f), split(wu[:, 2 * N:], head0_f)], axis=1),
                jnp.concatenate([zeros2, v01], axis=1)], axis=0), preferred_element_type=F32)
           for bot, wu, v01 in zip(bots, wus, v01s)]
    zs = [jnp.concatenate([wu, jnp.concatenate([zeros2[:L], vb], axis=1)], axis=0)
          for wu, vb in zip(wus, vbs)]
    mcs = [lax.dot_general(z, jnp.concatenate([it[2] * it[5], it[3] * it[5]], axis=0).astype(BF16), tn,
                           preferred_element_type=F32)
           for it, z in zip(items, zs)]
    return [(it[1] + ro[:, :2 * N], ro[:, 2 * N:],
             jnp.where(head0_f, mc[:N], mc[N:2 * N]), jnp.where(head0_f, mc[2 * N:3 * N], mc[3 * N:]))
            for it, ro, mc in zip(items, ros, mcs)]


def _rwkv_maps_kernel(L, rt_ref, at_ref, bt_ref, kt_ref, v_ref, pt_ref, rk_ref,
                      rhat_ref, ohat_ref, bonus_ref, m_ref, c_ref):
    N = RW_HEAD
    nck = rt_ref.shape[1] // L
    where = [(slice(c * L, (c + 1) * L), slice(p * LANES, (p + 1) * LANES), c)
             for c in range(nck) for p in range(rt_ref.shape[2] // LANES)]
    items = [(at_ref[0, rows, sl], rt_ref[0, rows, sl], bt_ref[0, rows, sl], kt_ref[0, rows, sl],
              v_ref[0, rows, sl], pt_ref[0, c, :, sl]) for rows, sl, c in where]
    maps = _rwkv_chunk_maps(L, items)
    head0 = lax.broadcasted_iota(jnp.int32, (1, LANES), 1) < N
    for (rows, sl, c), it, (rhat, ohat, m, cc) in zip(where, items, maps):
        _, rt, _, kt, v, _ = it
        rk = rt * kt * rk_ref[:, sl]
        total = jnp.sum(rk, axis=-1, keepdims=True)
        part = jnp.sum(jnp.where(head0, rk, 0.0), axis=-1, keepdims=True)
        rhat_ref[0, rows, sl] = rhat.astype(BF16)
        ohat_ref[0, rows, sl] = ohat
        bonus_ref[0, rows, sl] = jnp.where(head0, part, total - part) * v
        m_ref[0, c, :, sl] = m.astype(BF16)
        c_ref[0, c, :, sl] = cc


def _rwkv_maps(rt, at, bt, kt, v, pt, r_k):
    B, T, D = rt.shape
    L = min(CHUNK, T)
    ts = min(MAPS_TILE, T)
    nck = ts // L
    N = RW_HEAD
    tile = pl.BlockSpec((1, ts, MAPS_LANES), lambda b, h, c: (b, c, h))
    sq = pl.BlockSpec((1, nck, N, MAPS_LANES), lambda b, h, c: (b, c, 0, h))
    big = lambda dt: jax.ShapeDtypeStruct((B, T, D), dt)
    return pl.pallas_call(
        functools.partial(_rwkv_maps_kernel, L),
        grid=(B, D // MAPS_LANES, T // ts),
        in_specs=[tile] * 5 + [pl.BlockSpec((1, nck, 1, MAPS_LANES), lambda b, h, c: (b, c, 0, h)),
                               pl.BlockSpec((1, MAPS_LANES), lambda b, h, c: (0, h))],
        out_specs=[tile, tile, tile, sq, sq],
        out_shape=[big(BF16), big(F32), big(F32),
                   jax.ShapeDtypeStruct((B, T // L, N, D), BF16), jax.ShapeDtypeStruct((B, T // L, N, D), F32)],
        compiler_params=_params(("parallel", "parallel", "parallel")),
        name="rwkv_maps",
    )(rt, at, bt, kt, v, pt, r_k)


def _rwkv_state_kernel(L, rhat_ref, ohat_ref, bonus_ref, m_ref, c_ref, pt_ref, s0_ref, lnw_ref, lnb_ref,
                       o_ref, sT_ref, state):
    N = RW_HEAD
    nck = rhat_ref.shape[1] // L
    npairs = rhat_ref.shape[2] // LANES

    @pl.when(pl.program_id(1) == 0)
    def _():
        state[...] = s0_ref[0]

    nt = (((1,), (1,)), ((), ()))
    lane = lax.broadcasted_iota(jnp.int32, (1, LANES), 1)
    first = lane < N
    zero = jnp.zeros((), BF16)

    def block_diag(a):
        return jnp.concatenate([jnp.where(first, a, zero), jnp.where(first, zero, a)], axis=0)

    def head_sum(a):
        total = jnp.sum(a, axis=-1, keepdims=True)
        part = jnp.sum(jnp.where(first, a, 0.0), axis=-1, keepdims=True)
        return jnp.where(first, part, total - part)

    s = [state[:, p * LANES:(p + 1) * LANES] for p in range(npairs)]
    for c in range(nck):
        rows = slice(c * L, (c + 1) * L)
        os = []
        for p in range(npairs):
            pl_ = slice(p * LANES, (p + 1) * LANES)
            sb = s[p].astype(BF16)
            os.append(lax.dot_general(rhat_ref[0, rows, pl_], block_diag(sb), nt, preferred_element_type=F32)
                      + ohat_ref[0, rows, pl_])
            s[p] = (s[p] * pt_ref[0, c, :, pl_]
                    + jnp.dot(sb, block_diag(m_ref[0, c, :, pl_]), preferred_element_type=F32)
                    + c_ref[0, c, :, pl_])
        ds = [o - m for o, m in zip(os, [head_sum(o) * (1.0 / N) for o in os])]
        variances = [head_sum(d * d) * (1.0 / N) for d in ds]
        for p, (d, var) in enumerate(zip(ds, variances)):
            pl_ = slice(p * LANES, (p + 1) * LANES)
            o_ref[0, rows, pl_] = (d * lax.rsqrt(var + GN_EPS) * lnw_ref[:, pl_] + lnb_ref[:, pl_]
                                   + bonus_ref[0, rows, pl_])
    for p in range(npairs):
        state[:, p * LANES:(p + 1) * LANES] = s[p]

    @pl.when(pl.program_id(1) == pl.num_programs(1) - 1)
    def _():
        sT_ref[0] = state[...]


def _rwkv_state(rhat, ohat, bonus, m, c, pt, s0, p):
    B, T, D = ohat.shape
    L = min(CHUNK, T)
    ts = min(STATE_TILE, T)
    nck = ts // L
    N = RW_HEAD
    H = D // N
    tile = pl.BlockSpec((1, ts, D), lambda b, t: (b, t, 0))
    sq = pl.BlockSpec((1, nck, N, D), lambda b, t: (b, t, 0, 0))
    st = pl.BlockSpec((1, N, D), lambda b, t: (b, 0, 0))
    s0_vk = jnp.transpose(s0, (0, 2, 1, 3)).reshape(B, N, D)
    o, sT = pl.pallas_call(
        functools.partial(_rwkv_state_kernel, L),
        grid=(B, T // ts),
        in_specs=[tile, tile, tile, sq, sq, pl.BlockSpec((1, nck, 1, D), lambda b, t: (b, t, 0, 0)), st,
                  _const_spec((1, D)), _const_spec((1, D))],
        out_specs=[tile, st],
        out_shape=[jax.ShapeDtypeStruct((B, T, D), F32), jax.ShapeDtypeStruct((B, N, D), F32)],
        scratch_shapes=[pltpu.VMEM((N, D), F32)],
        compiler_params=_params(("parallel", "arbitrary")),
        name="rwkv_state",
    )(rhat, ohat, bonus, m, c, pt, s0_vk, p['ln_w'], p['ln_b'])
    return o, jnp.transpose(sT.reshape(B, N, H, N), (0, 2, 1, 3))


def _rwkv_scan(rt, at, bt, kt, v, pt, s0, p):
    rhat, ohat, bonus, m, c = _rwkv_maps(rt, at, bt, kt, v, pt, p['r_k'])
    return _rwkv_state(rhat, ohat, bonus, m, c, pt, s0, p)


def _post_kernel(x_ref, a_ref, b_ref, wo_ref, gmix_ref, gpre_ref, wgu_ref, wd_ref, gpost_ref, o_ref):
    f = wd_ref.shape[0]
    x = x_ref[...] + _rms(_mm(a_ref[...] * b_ref[...], wo_ref[...]), gmix_ref[...])
    gu = _mm(_rms(x, gpre_ref[...]), wgu_ref[...])
    gate = gu[:, :f]
    act = gate * jax.nn.sigmoid(gate) * gu[:, f:]
    o_ref[...] = x + _rms(_mm(act, wd_ref[...]), gpost_ref[...])


def _post(x, a, b, wo, lp):
    R, D = x.shape
    tm = min(TOKEN_TILE, R)
    tile = pl.BlockSpec((tm, D), lambda t: (t, 0))
    consts = (wo, lp['g_mix'], lp['g_ffn_pre'], lp['ffn_w_gu'], lp['ffn_w_down'], lp['g_ffn_post'])
    return pl.pallas_call(
        _post_kernel,
        grid=(R // tm,),
        in_specs=[tile, tile, tile] + [_const_spec(c.shape) for c in consts],
        out_specs=tile,
        out_shape=jax.ShapeDtypeStruct((R, D), F32),
        compiler_params=_params(("parallel",)),
        name="post",
    )(x, a, b, *consts)


def _mlstm_pre_kernel(L, nq, x_ref, g0_ref, win_ref, wgc_ref, wgr_ref, bc_ref, br_ref,
                      q_ref, k_ref, v_ref, og_ref, gc_ref, gr_ref):
    h = _rms(x_ref[...], g0_ref[...]).astype(BF16)
    z = jnp.dot(h, win_ref[...], preferred_element_type=F32)
    nv = v_ref.shape[1]
    dqk = nq // ML_HEADS
    q_ref[...] = z[:, :nq] * (dqk ** -0.5)
    k_ref[...] = z[:, nq:2 * nq]
    v_ref[...] = z[:, 2 * nq:2 * nq + nv]
    og_ref[...] = jax.nn.sigmoid(z[:, 2 * nq + nv:])

    def cap(t):
        return GATE_CAP * jnp.tanh(t / GATE_CAP)

    gc = cap(jnp.dot(h, wgc_ref[...], preferred_element_type=F32) + bc_ref[...])
    lane = lax.broadcasted_iota(jnp.int32, gc.shape, 1)
    gc_ref[...] = jnp.where(lane < ML_HEADS, gc, jax.nn.log_sigmoid(gc))
    gr = cap(lax.dot_general(wgr_ref[...], h, (((1,), (1,)), ((), ())), preferred_element_type=F32)
             + br_ref[...])
    rowi = lax.broadcasted_iota(jnp.int32, gr.shape, 0)
    gr = jnp.where(rowi < ML_HEADS, gr, jax.nn.log_sigmoid(gr))
    for c in range(gr_ref.shape[0]):
        gr_ref[c] = gr[:, c * L:(c + 1) * L]


def _mlstm_pre(x, L, p):
    R, D = x.shape
    tm = min(TOKEN_TILE, R)
    nq = ML_HEADS * (D // (2 * ML_HEADS))
    nv = D
    consts = [p['g0'], p['w_in'], p['w_gc'], p['w_gr'], p['b_c'], p['b_r']]
    row = lambda w: pl.BlockSpec((tm, w), lambda t: (t, 0))
    return pl.pallas_call(
        functools.partial(_mlstm_pre_kernel, L, nq),
        grid=(R // tm,),
        in_specs=[row(D)] + [_const_spec(c.shape) for c in consts],
        out_specs=[row(nq), row(nq), row(nv), row(nv), row(LANES),
                   pl.BlockSpec((tm // L, GATE_ROWS, L), lambda t: (t, 0, 0))],
        out_shape=[jax.ShapeDtypeStruct((R, nq), F32), jax.ShapeDtypeStruct((R, nq), F32),
                   jax.ShapeDtypeStruct((R, nv), F32), jax.ShapeDtypeStruct((R, nv), F32),
                   jax.ShapeDtypeStruct((R, LANES), F32),
                   jax.ShapeDtypeStruct((R // L, GATE_ROWS, L), F32)],
        compiler_params=_params(("parallel",)),
        name="mlstm_pre",
    )(x, *consts)


def _mlstm_scan_kernel(L, q_ref, k_ref, v_ref, gc_ref, gr_ref, c0_ref, n0_ref, m0_ref, hnw_ref,
                       h_ref, cT_ref, nT_ref, mT_ref, c_st, n_st, m_st):
    H = ML_HEADS
    nck = q_ref.shape[0] // L
    dqk = q_ref.shape[1] // H
    dv = v_ref.shape[1] // H

    @pl.when(pl.program_id(1) == 0)
    def _():
        c_st[...] = c0_ref[0]
        n_st[...] = n0_ref[0]
        m_st[...] = m0_ref[0]

    row = lax.broadcasted_iota(jnp.int32, (L, L), 0)
    col = lax.broadcasted_iota(jnp.int32, (L, L), 1)
    causal = row >= col
    tril = causal.astype(BF16)
    triu = (row <= col).astype(BF16)
    nt = (((1,), (1,)), ((), ()))
    tn = (((0,), (0,)), ((), ()))

    gcs = [gc_ref[c * L:(c + 1) * L, :] for c in range(nck)]
    grs = [gr_ref[c] for c in range(nck)]
    bcs = [_dot_exact_lhs(tril, gc) for gc in gcs]
    brs = []
    for gr in grs:
        hi, mid, lo = _split3(gr)
        brs.append(jnp.dot(hi, triu, preferred_element_type=F32) + jnp.dot(mid, triu, preferred_element_type=F32)
                   + jnp.dot(lo, triu, preferred_element_type=F32))

    where = [(c, hd) for c in range(nck) for hd in range(H)]
    rows_of = [slice(c * L, (c + 1) * L) for c, _ in where]
    qs = [q_ref[rows, hd * dqk:(hd + 1) * dqk] for rows, (_, hd) in zip(rows_of, where)]
    ks = [k_ref[rows, hd * dqk:(hd + 1) * dqk] for rows, (_, hd) in zip(rows_of, where)]
    vbs = [v_ref[rows, hd * dv:(hd + 1) * dv].astype(BF16) for rows, (_, hd) in zip(rows_of, where)]
    li_cs = [jnp.broadcast_to(gcs[c][:, hd:hd + 1], (L, dqk)) for c, hd in where]
    b_cs = [jnp.broadcast_to(bcs[c][:, H + hd:H + hd + 1], (L, dqk)) for c, hd in where]
    dmats = [jnp.where(causal, b_c[:, :L] - brs[c][H + hd:H + hd + 1, :] + grs[c][hd:hd + 1, :], -jnp.inf)
             for (c, hd), b_c in zip(where, b_cs)]
    intras = [jnp.broadcast_to(jnp.max(dmat, axis=-1, keepdims=True), (L, dqk)) for dmat in dmats]
    gs = [intra - b_c for intra, b_c in zip(intras, b_cs)]
    g_ends = [g[L - 1:L, :] for g in gs]
    g_lasts = [g_end[:, :1] for g_end in g_ends]
    b_lasts = [b_c[L - 1:L, :1] for b_c in b_cs]
    es = [jnp.exp(dmat - intra[:, :L]) for dmat, intra in zip(dmats, intras)]
    kws = [k * jnp.exp(li_c - b_c - g_end) for k, li_c, b_c, g_end in zip(ks, li_cs, b_cs, g_ends)]
    qbs = [q.astype(BF16) for q in qs]
    qks = [lax.dot_general(qb, k.astype(BF16), nt, preferred_element_type=F32) * e
           for qb, k, e in zip(qbs, ks, es)]
    num_i = [_mm(qk, vb) for qk, vb in zip(qks, vbs)]
    kvs = [lax.dot_general(kw.astype(BF16), vb, tn, preferred_element_type=F32) for kw, vb in zip(kws, vbs)]
    ksums = [jnp.sum(kw, axis=0, keepdims=True) for kw in kws]
    qk_sums = [jnp.sum(qk, axis=-1, keepdims=True) for qk in qks]

    c_cur = [c_st[hd] for hd in range(H)]
    n_cur = [n_st[hd] for hd in range(H)]
    m_cur = [m_st[hd] for hd in range(H)]
    c_prevs, n_prevs, m_prevs = [], [], []
    for (c, hd), g_last, b_last, kv, ksum in zip(where, g_lasts, b_lasts, kvs, ksums):
        c_prevs.append(c_cur[hd].astype(BF16))
        n_prevs.append(n_cur[hd])
        m_prevs.append(m_cur[hd])
        mx = jnp.maximum(m_cur[hd], g_last)
        e_old = jnp.exp(m_cur[hd] - mx)
        e_new = jnp.exp(g_last - mx)
        c_cur[hd] = e_old * c_cur[hd] + e_new * kv
        n_cur[hd] = e_old * n_cur[hd] + e_new * ksum
        m_cur[hd] = b_last + mx
    for hd in range(H):
        c_st[hd] = c_cur[hd]
        n_st[hd] = n_cur[hd]
        m_st[hd] = m_cur[hd]

    qcs = [_mm(qb, cp) for qb, cp in zip(qbs, c_prevs)]
    qn_sums = [jnp.sum(q * n_prev, axis=-1, keepdims=True) for q, n_prev in zip(qs, n_prevs)]
    mxs = [jnp.maximum(m_prev, g) for g, m_prev in zip(gs, m_prevs)]
    f_news = [jnp.exp(g - mx) for g, mx in zip(gs, mxs)]
    f_olds = [jnp.exp(m_prev - mx) for mx, m_prev in zip(mxs, m_prevs)]
    floors = [jnp.exp(-(b_c + mx)) for b_c, mx in zip(b_cs, mxs)]
    invs = [1.0 / jnp.maximum(jnp.abs(f_new * qk_sum + f_old * qn_sum), floor)
            for f_new, f_old, qk_sum, qn_sum, floor in zip(f_news, f_olds, qk_sums, qn_sums, floors)]
    wide = lambda a: jnp.concatenate([a] * (dv // dqk), axis=1)
    hs = [wide(f_new * inv) * ni + wide(f_old * inv) * qc
          for f_new, f_old, inv, ni, qc in zip(f_news, f_olds, invs, num_i, qcs)]
    h_ms = [jnp.mean(h * h, axis=-1, keepdims=True) for h in hs]
    for rows, (_, hd), h, h_m in zip(rows_of, where, hs, h_ms):
        h_ref[rows, hd * dv:(hd + 1) * dv] = h * lax.rsqrt(h_m + EPS) * hnw_ref[:, hd * dv:(hd + 1) * dv]

    @pl.when(pl.program_id(1) == pl.num_programs(1) - 1)
    def _():
        cT_ref[0] = c_st[...]
        nT_ref[0] = n_st[...]
        mT_ref[0] = m_st[...]


def _mlstm_scan(q, k, v, gc, gr, c0, n0, m0, hn_w, B, T):
    L = min(CHUNK, T)
    ts = min(MLSTM_TILE, T)
    nc = T // ts
    H = ML_HEADS
    nq, nv = q.shape[1], v.shape[1]
    dqk, dv = nq // H, nv // H
    row = lambda w: pl.BlockSpec((ts, w), lambda b, c: (b * nc + c, 0))
    st = lambda *s: pl.BlockSpec((1,) + s, lambda b, c: (b,) + (0,) * len(s))
    return pl.pallas_call(
        functools.partial(_mlstm_scan_kernel, L),
        grid=(B, nc),
        in_specs=[row(nq), row(nq), row(nv), row(LANES),
                  pl.BlockSpec((ts // L, GATE_ROWS, L), lambda b, c: (b * nc + c, 0, 0)),
                  st(H, dqk, dv), st(H, 1, dqk), st(H, 1, 1), _const_spec(hn_w.shape)],
        out_specs=[row(nv), st(H, dqk, dv), st(H, 1, dqk), st(H, 1, 1)],
        out_shape=[jax.ShapeDtypeStruct((B * T, nv), F32), jax.ShapeDtypeStruct((B, H, dqk, dv), F32),
                   jax.ShapeDtypeStruct((B, H, 1, dqk), F32), jax.ShapeDtypeStruct((B, H, 1, 1), F32)],
        scratch_shapes=[pltpu.VMEM((H, dqk, dv), F32), pltpu.VMEM((H, 1, dqk), F32),
                        pltpu.VMEM((H, 1, 1), F32)],
        compiler_params=_params(("parallel", "arbitrary")),
        name="mlstm_scan",
    )(q, k, v, gc, gr, c0, n0.reshape(B, H, 1, dqk), m0.reshape(B, H, 1, 1), hn_w)


def _prep_weights(w):
    D = w['norm_g'].shape[-1]
    H = D // RW_HEAD
    row = lambda a: a.reshape(1, -1).astype(F32)
    head_of_lane = jnp.arange(D) // RW_HEAD
    hsum = (head_of_lane[:, None] == jnp.arange(LANES)[None, :]).astype(BF16)
    hexp = hsum.T
    layers = []
    for i in range(DEPTH):
        j = i // N_MIXERS
        ng = w['norm_g'][i]
        lp = {'g0': row(ng[0]), 'g_mix': row(ng[1]), 'g_ffn_pre': row(ng[2]), 'g_ffn_post': row(ng[3]),
              'ffn_w_gu': w['ffn_w_gu'][i].astype(BF16), 'ffn_w_down': w['ffn_w_down'][i].astype(BF16)}
        if i % N_MIXERS == 0:
            lp.update({
                'mu': jnp.pad(w['rw_mu'][j], ((0, 2), (0, 0))),
                'w_rkv': w['rw_w_rkv'][j].astype(BF16),
                'w0': row(w['rw_w0'][j]), 'w1': w['rw_w1'][j].astype(BF16), 'w2': w['rw_w2'][j].astype(BF16),
                'a0': row(w['rw_a0'][j]), 'a1': w['rw_a1'][j].astype(BF16), 'a2': w['rw_a2'][j].astype(BF16),
                'g1': w['rw_g1'][j].astype(BF16), 'g2': w['rw_g2'][j].astype(BF16),
                'k_k': row(w['rw_k_k'][j]), 'k_a': row(w['rw_k_a'][j]), 'r_k': row(w['rw_r_k'][j]),
                'ln_w': row(w['rw_ln_w'][j]), 'ln_b': row(w['rw_ln_b'][j]),
                'w_o': w['rw_w_o'][j].astype(BF16), 'hsum': hsum, 'hexp': hexp,
            })
            if j > 0:
                lp.update({'v0': row(w['rw_v0'][j - 1]), 'v1': w['rw_v1'][j - 1].astype(BF16),
                           'v2': w['rw_v2'][j - 1].astype(BF16)})
        else:
            w_in = w['ml_w_in'][j]
            n_main = w_in.shape[1] - 2 * ML_HEADS
            w_g = w_in[:, n_main:]
            bias = w['ml_b_gates'][j].astype(F32)
            lp.update({
                'w_in': w_in[:, :n_main].astype(BF16),
                'w_gc': jnp.pad(w_g, ((0, 0), (0, LANES - 2 * ML_HEADS))).astype(BF16),
                'w_gr': jnp.pad(w_g.T, ((0, GATE_ROWS - 2 * ML_HEADS), (0, 0))).astype(BF16),
                'b_c': jnp.pad(bias, (0, LANES - 2 * ML_HEADS)).reshape(1, LANES),
                'b_r': jnp.pad(bias, (0, GATE_ROWS - 2 * ML_HEADS)).reshape(GATE_ROWS, 1),
                'hn_w': row(w['ml_hn_w'][j]), 'w_out': w['ml_w_out'][j].astype(BF16),
            })
        layers.append(lp)
    return layers


def _trunk(x, rw_shift, rw_S, ml_C, ml_n, ml_m, layers):
    B, T, D = x.shape
    L = min(CHUNK, T)
    flat = lambda a: a.reshape(B * T, a.shape[-1])
    v_first = None
    shifts, Ss, Cs, ns, ms = [], [], [], [], []
    xf = flat(x)
    for i, lp in enumerate(layers):
        j = i // N_MIXERS
        if i % N_MIXERS == 0:
            rt, at, bt, kt, v, g, pt, sh = _rwkv_pre(xf.reshape(B, T, D), rw_shift[j], lp, j, v_first)
            if j == 0:
                v_first = v
            o, S = _rwkv_scan(rt, at, bt, kt, v, pt, rw_S[j], lp)
            shifts.append(sh)
            Ss.append(S)
            xf = _post(xf, flat(o), flat(g), lp['w_o'], lp)
        else:
            q, k, v, og, gc, gr = _mlstm_pre(xf, L, lp)
            hn, C, n, m = _mlstm_scan(q, k, v, gc, gr, ml_C[j], ml_n[j], ml_m[j], lp['hn_w'], B, T)
            Cs.append(C)
            ns.append(n.reshape(B, ML_HEADS, -1))
            ms.append(m.reshape(B, ML_HEADS))
            xf = _post(xf, og, hn, lp['w_out'], lp)
    return xf.reshape(B, T, D), jnp.stack(shifts), jnp.stack(Ss), jnp.stack(Cs), jnp.stack(ns), jnp.stack(ms)


def kernel(x_prompt, x_sample, state_rwkv_shift, state_rwkv_S, state_mlstm_C, state_mlstm_n, state_mlstm_m,
           norm_g, rw_mu, rw_w_rkv, rw_w0, rw_w1, rw_w2, rw_a0, rw_a1, rw_a2, rw_v0, rw_v1, rw_v2,
           rw_g1, rw_g2, rw_k_k, rw_k_a, rw_r_k, rw_ln_w, rw_ln_b, rw_w_o,
           ml_w_in, ml_b_gates, ml_hn_w, ml_w_out, ffn_w_gu, ffn_w_down):
    w = {
        'norm_g': norm_g, 'rw_mu': rw_mu, 'rw_w_rkv': rw_w_rkv, 'rw_w0': rw_w0, 'rw_w1': rw_w1,
        'rw_w2': rw_w2, 'rw_a0': rw_a0, 'rw_a1': rw_a1, 'rw_a2': rw_a2, 'rw_v0': rw_v0, 'rw_v1': rw_v1,
        'rw_v2': rw_v2, 'rw_g1': rw_g1, 'rw_g2': rw_g2, 'rw_k_k': rw_k_k, 'rw_k_a': rw_k_a,
        'rw_r_k': rw_r_k, 'rw_ln_w': rw_ln_w, 'rw_ln_b': rw_ln_b, 'rw_w_o': rw_w_o,
        'ml_w_in': ml_w_in, 'ml_b_gates': ml_b_gates, 'ml_hn_w': ml_hn_w, 'ml_w_out': ml_w_out,
        'ffn_w_gu': ffn_w_gu, 'ffn_w_down': ffn_w_down,
    }
    layers = _prep_weights(w)
    Bp = x_prompt.shape[0]
    zeros = lambda a: jnp.zeros((a.shape[0], Bp) + a.shape[2:], F32)
    out_p = _trunk(x_prompt, zeros(state_rwkv_shift), zeros(state_rwkv_S), zeros(state_mlstm_C),
                   zeros(state_mlstm_n), zeros(state_mlstm_m), layers)
    out_s = _trunk(x_sample, state_rwkv_shift, state_rwkv_S, state_mlstm_C, state_mlstm_n, state_mlstm_m,
                   layers)
    return (out_p[0], out_s[0]) + tuple(out_p[1:]) + tuple(out_s[1:])
```

```python
import functools
import math

import jax
import jax.numpy as jnp
from jax import lax
from jax.experimental import pallas as pl
from jax.experimental.pallas import tpu as pltpu

F32 = jnp.float32
BF16 = jnp.bfloat16

DEPTH = 4
N_MIXERS = 2
CHUNK = 64
RW_HEAD = 64
ML_HEADS = 4
GN_EPS = 64e-5
GATE_CAP = 15.0
EPS = 1e-6

LANES = 128
VMEM_LIMIT = 56 * 1024 * 1024
TOKEN_TILE = 512
POST_ROWS = 256
PRE_ROWS = 128
RWKV_PRE_TILE = 256
MAPS_TILE = 512
MAPS_LANES = 512
STATE_TILE = 256
MLSTM_TILE = 256
GATE_ROWS = 16


def _const_spec(shape):
    nd = len(shape)
    return pl.BlockSpec(shape, lambda *_: (0,) * nd, pipeline_mode=pl.Buffered(1))


def _params(sem, vmem=VMEM_LIMIT):
    return pltpu.CompilerParams(dimension_semantics=sem, vmem_limit_bytes=vmem)


def _rms(x, g):
    return x * lax.rsqrt(jnp.mean(x * x, axis=-1, keepdims=True) + EPS) * g


def _mm(a, w):
    return jnp.dot(a.astype(BF16), w, preferred_element_type=F32)


def _split2(x):
    hi = x.astype(BF16)
    lo = (x - hi.astype(F32)).astype(BF16)
    return hi, lo


def _split3(x):
    hi = x.astype(BF16)
    r1 = x - hi.astype(F32)
    mid = r1.astype(BF16)
    lo = (r1 - mid.astype(F32)).astype(BF16)
    return hi, mid, lo


def _dot_exact_rhs(x, w01):
    hi, lo = _split2(x)
    return (jnp.dot(hi, w01, preferred_element_type=F32)
            + jnp.dot(lo, w01, preferred_element_type=F32))


def _dot_exact_lhs(w01, x):
    hi, mid, lo = _split3(x)
    return (jnp.dot(w01, hi, preferred_element_type=F32)
            + jnp.dot(w01, mid, preferred_element_type=F32)
            + jnp.dot(w01, lo, preferred_element_type=F32))


def _rwkv_pre_kernel(has_vfirst, nct, *refs):
    (x_ref, shift0_ref, g0_ref, mu_ref, wrkv_ref, w0_ref, w1_ref, w2_ref, a0_ref, a1_ref, a2_ref,
     g1_ref, g2_ref, kk_ref, ka_ref, hsum_ref, hexp_ref, tri_ref, sel_ref) = refs[:19]
    refs = refs[19:]
    if has_vfirst:
        v0_ref, v1_ref, v2_ref, vfirst_ref = refs[:4]
        refs = refs[4:]
    rt_ref, at_ref, bt_ref, kt_ref, v_ref, g_ref, pt_ref, shift_ref, carry = refs

    tm = x_ref.shape[1]

    @pl.when(pl.program_id(1) == 0)
    def _():
        carry[...] = shift0_ref[0]

    h = _rms(x_ref[0], g0_ref[...])
    row = lax.broadcasted_iota(jnp.int32, h.shape, 0)
    prev = jnp.where(row == 0, carry[...], pltpu.roll(h, 1, axis=0))
    last = h[tm - 1:tm, :]
    carry[...] = last
    shift_ref[0] = last

    rows = min(PRE_ROWS, tm)
    groups = [slice(i * rows, (i + 1) * rows) for i in range(tm // rows)]
    hs = [h[r, :] for r in groups]
    dxs = [prev[r, :] - hg for r, hg in zip(groups, hs)]
    mixed = [[(hg + dx * mu_ref[c:c + 1, :]).astype(BF16) for c in range(6)] for hg, dx in zip(hs, dxs)]
    dot = lambda a, w: jnp.dot(a, w, preferred_element_type=F32)
    rs = [dot(m[0], wrkv_ref[0]) for m in mixed]
    ks = [dot(m[2], wrkv_ref[1]) for m in mixed]
    vs = [dot(m[3], wrkv_ref[2]) for m in mixed]
    w_lo = [jnp.tanh(dot(m[1], w1_ref[...])) for m in mixed]
    a_lo = [dot(m[4], a1_ref[...]) for m in mixed]
    g_lo = [jax.nn.sigmoid(dot(m[5], g1_ref[...])) for m in mixed]
    lws = [(-math.exp(-0.5)) * jax.nn.sigmoid(w0_ref[...] + _mm(t, w2_ref[...])) for t in w_lo]
    as_ = [jax.nn.sigmoid(a0_ref[...] + _mm(t, a2_ref[...])) for t in a_lo]
    gs = [_mm(t, g2_ref[...]) for t in g_lo]
    if has_vfirst:
        v_lo = [dot(m[3], v1_ref[...]) for m in mixed]
        mixes = [jax.nn.sigmoid(v0_ref[...] + _mm(t, v2_ref[...])) for t in v_lo]
        vs = [v + (vfirst_ref[0, r, :] - v) * mix for r, v, mix in zip(groups, vs, mixes)]

    kkrs = [k * kk_ref[...] for k in ks]
    ssqs = [_mm(kkr * kkr, hsum_ref[...]) for kkr in kkrs]
    invs = [lax.rsqrt(jnp.maximum(ssq, 1e-24)) for ssq in ssqs]
    kks = [kkr * _dot_exact_rhs(inv, hexp_ref[...]) for kkr, inv in zip(kkrs, invs)]
    kmods = [k * (1.0 + (a - 1.0) * ka_ref[...]) for k, a in zip(ks, as_)]

    splits = [_split2(lw) for lw in lws]
    tri = tri_ref[:rows, :rows]
    bs = [dot(tri, hi) + dot(tri, lo) for hi, lo in splits]
    pt_log = sum(dot(sel_ref[:, r], hi) + dot(sel_ref[:, r], lo) for r, (hi, lo) in zip(groups, splits))
    for r, rr, k_mod, kk, a, v, g, lw, b in zip(groups, rs, kmods, kks, as_, vs, gs, lws, bs):
        enb = jnp.exp(-b)
        rt_ref[0, r, :] = rr * jnp.exp(b)
        at_ref[0, r, :] = -kk * jnp.exp(b - lw)
        bt_ref[0, r, :] = kk * a * enb
        kt_ref[0, r, :] = k_mod * enb
        v_ref[0, r, :] = v
        g_ref[0, r, :] = g
    pt_ref[0, 0] = jnp.exp(pt_log)[:nct]


def _rwkv_pre(x, shift0, p, j, v_first):
    B, T, D = x.shape
    L = min(CHUNK, T)
    tm = min(RWKV_PRE_TILE, T)
    nct = tm // L
    has_vfirst = v_first is not None
    row = jnp.arange(tm)
    tri = ((row[:, None] >= row[None, :]) & (row[:, None] // L == row[None, :] // L)).astype(BF16)
    sel = (jnp.arange(GATE_ROWS)[:, None] == row[None, :] // L).astype(BF16)

    tile = pl.BlockSpec((1, tm, D), lambda b, t: (b, t, 0))
    consts = [p['g0'], p['mu'], p['w_rkv'], p['w0'], p['w1'], p['w2'], p['a0'], p['a1'], p['a2'],
              p['g1'], p['g2'], p['k_k'], p['k_a'], p['hsum'], p['hexp'], tri, sel]
    args = [x, shift0.reshape(B, 1, D)] + consts
    specs = [tile, pl.BlockSpec((1, 1, D), lambda b, t: (b, 0, 0))] + [_const_spec(c.shape) for c in consts]
    if has_vfirst:
        vc = [p['v0'], p['v1'], p['v2']]
        args += vc + [v_first]
        specs += [_const_spec(c.shape) for c in vc] + [tile]

    big = jax.ShapeDtypeStruct((B, T, D), F32)
    outs = pl.pallas_call(
        functools.partial(_rwkv_pre_kernel, has_vfirst, nct),
        grid=(B, T // tm),
        in_specs=specs,
        out_specs=[tile] * 6 + [pl.BlockSpec((1, 1, nct, D), lambda b, t: (b, t, 0, 0)),
                                pl.BlockSpec((1, 1, D), lambda b, t: (b, 0, 0))],
        out_shape=[big] * 6 + [jax.ShapeDtypeStruct((B, T // tm, nct, D), F32),
                               jax.ShapeDtypeStruct((B, 1, D), F32)],
        scratch_shapes=[pltpu.VMEM((1, D), F32)],
        compiler_params=_params(("parallel", "arbitrary")),
        name="rwkv_pre",
    )(*args)
    rt, at, bt, kt, v, g, pt, shift = outs
    return rt, at, bt, kt, v, g, pt.reshape(B, T // L, 1, D), shift.reshape(B, D)


def _rwkv_chunk_maps(L, items):
    N = RW_HEAD
    nt = (((1,), (1,)), ((), ()))
    tn = (((0,), (0,)), ((), ()))
    head0_f = lax.broadcasted_iota(jnp.int32, (1, 2 * N), 1) < N
    head0_t = lax.broadcasted_iota(jnp.int32, (1, 2 * L), 1) < L
    zero = jnp.zeros((), BF16)

    def split(z, first):
        return jnp.concatenate([jnp.where(first, z, zero), jnp.where(first, zero, z)], axis=0)

    row = lax.broadcasted_iota(jnp.int32, (L, 4 * L), 0)
    col = lax.broadcasted_iota(jnp.int32, (L, 4 * L), 1) % L
    strict = row > col
    incl = row >= col
    eye = jnp.where(lax.broadcasted_iota(jnp.int32, (L, 2 * L), 0)
                    == lax.broadcasted_iota(jnp.int32, (L, 2 * L), 1) % L, 1.0, 0.0)

    ats = [it[0].astype(BF16) for it in items]
    vbs = [it[4].astype(BF16) for it in items]
    v01s = [split(vb, head0_f) for vb in vbs]
    gs = [lax.dot_general(jnp.concatenate([at, it[1].astype(BF16)], axis=0),
                          jnp.concatenate([split(it[2].astype(BF16), head0_f),
                                           split(it[3].astype(BF16), head0_f)], axis=0),
                          nt, preferred_element_type=F32)
          for it, at in zip(items, ats)]
    tops = [jnp.where(strict, g[:L], 0.0) for g in gs]
    bots = [jnp.where(incl, g[L:], 0.0).astype(BF16) for g in gs]
    avs = [_mm(top[:, 2 * L:], v01) for top, v01 in zip(tops, v01s)]
    n_pows = [top[:, :2 * L].astype(BF16) for top in tops]
    ts = [eye + top[:, :2 * L] for top in tops]
    n_pows = [jnp.dot(n, split(n, head0_t), preferred_element_type=F32).astype(BF16) for n in n_pows]
    span = 2
    while True:
        span *= 2
        if span >= L:
            ts = [t + jnp.dot(n, split(t.astype(BF16), head0_t), preferred_element_type=F32)
                  for t, n in zip(ts, n_pows)]
            break
        res = [jnp.dot(n, jnp.concatenate([split(t.astype(BF16), head0_t), split(n, head0_t)], axis=1),
                       preferred_element_type=F32) for t, n in zip(ts, n_pows)]
        ts = [t + r[:, :2 * L] for t, r in zip(ts, res)]
        n_pows = [r[:, 2 * L:].astype(BF16) for r in res]

    wus = [_mm(t, jnp.concatenate([split(at, head0_f), split(av.astype(BF16), head0_f)], axis=1)).astype(BF16)
           for t, at, av in zip(ts, ats, avs)]
    zeros2 = jnp.zeros((2 * L, 2 * N), BF16)
    ros = [jnp.dot(bot, jnp.concatenate(
               [jnp.concatenate([split(wu[:, :2 * N], head0_f), split(wu[:, 2 * N:], head0_f)], axis=1),
                jnp.concatenate([zeros2, v01], axis=1)], axis=0), preferred_element_type=F32)
           for bot, wu, v01 in zip(bots, wus, v01s)]
    zs = [jnp.concatenate([wu, jnp.concatenate([zeros2[:L], vb], axis=1)], axis=0)
          for wu, vb in zip(wus, vbs)]
    mcs = [lax.dot_general(z, jnp.concatenate([it[2] * it[5], it[3] * it[5]], axis=0).astype(BF16), tn,
                           preferred_element_type=F32)
           for it, z in zip(items, zs)]
    return [(it[1] + ro[:, :2 * N], ro[:, 2 * N:],
             jnp.where(head0_f, mc[:N], mc[N:2 * N]), jnp.where(head0_f, mc[2 * N:3 * N], mc[3 * N:]))
            for it, ro, mc in zip(items, ros, mcs)]


def _rwkv_maps_kernel(L, rt_ref, at_ref, bt_ref, kt_ref, v_ref, pt_ref, rk_ref,
                      rhat_ref, ohat_ref, bonus_ref, m_ref, c_ref):
    N = RW_HEAD
    nck = rt_ref.shape[1] // L
    where = [(slice(c * L, (c + 1) * L), slice(p * LANES, (p + 1) * LANES), c)
             for c in range(nck) for p in range(rt_ref.shape[2] // LANES)]
    items = [(at_ref[0, rows, sl], rt_ref[0, rows, sl], bt_ref[0, rows, sl], kt_ref[0, rows, sl],
              v_ref[0, rows, sl], pt_ref[0, c, :, sl]) for rows, sl, c in where]
    maps = _rwkv_chunk_maps(L, items)
    head0 = lax.broadcasted_iota(jnp.int32, (1, LANES), 1) < N
    for (rows, sl, c), it, (rhat, ohat, m, cc) in zip(where, items, maps):
        _, rt, _, kt, v, _ = it
        rk = rt * kt * rk_ref[:, sl]
        total = jnp.sum(rk, axis=-1, keepdims=True)
        part = jnp.sum(jnp.where(head0, rk, 0.0), axis=-1, keepdims=True)
        rhat_ref[0, rows, sl] = rhat.astype(BF16)
        ohat_ref[0, rows, sl] = ohat
        bonus_ref[0, rows, sl] = jnp.where(head0, part, total - part) * v
        m_ref[0, c, :, sl] = m.astype(BF16)
        c_ref[0, c, :, sl] = cc


def _rwkv_maps(rt, at, bt, kt, v, pt, r_k):
    B, T, D = rt.shape
    L = min(CHUNK, T)
    ts = min(MAPS_TILE, T)
    nck = ts // L
    N = RW_HEAD
    tile = pl.BlockSpec((1, ts, MAPS_LANES), lambda b, h, c: (b, c, h))
    sq = pl.BlockSpec((1, nck, N, MAPS_LANES), lambda b, h, c: (b, c, 0, h))
    big = lambda dt: jax.ShapeDtypeStruct((B, T, D), dt)
    return pl.pallas_call(
        functools.partial(_rwkv_maps_kernel, L),
        grid=(B, D // MAPS_LANES, T // ts),
        in_specs=[tile] * 5 + [pl.BlockSpec((1, nck, 1, MAPS_LANES), lambda b, h, c: (b, c, 0, h)),
                               pl.BlockSpec((1, MAPS_LANES), lambda b, h, c: (0, h))],
        out_specs=[tile, tile, tile, sq, sq],
        out_shape=[big(BF16), big(F32), big(F32),
                   jax.ShapeDtypeStruct((B, T // L, N, D), BF16), jax.ShapeDtypeStruct((B, T // L, N, D), F32)],
        compiler_params=_params(("parallel", "parallel", "parallel")),
        name="rwkv_maps",
    )(rt, at, bt, kt, v, pt, r_k)


def _rwkv_state_kernel(L, rhat_ref, ohat_ref, bonus_ref, m_ref, c_ref, pt_ref, s0_ref, lnw_ref, lnb_ref,
                       o_ref, sT_ref, state):
    N = RW_HEAD
    nck = rhat_ref.shape[1] // L
    npairs = rhat_ref.shape[2] // LANES

    @pl.when(pl.program_id(1) == 0)
    def _():
        state[...] = s0_ref[0]

    nt = (((1,), (1,)), ((), ()))
    lane = lax.broadcasted_iota(jnp.int32, (1, LANES), 1)
    first = lane < N
    zero = jnp.zeros((), BF16)

    def block_diag(a):
        return jnp.concatenate([jnp.where(first, a, zero), jnp.where(first, zero, a)], axis=0)

    def head_sum(a):
        total = jnp.sum(a, axis=-1, keepdims=True)
        part = jnp.sum(jnp.where(first, a, 0.0), axis=-1, keepdims=True)
        return jnp.where(first, part, total - part)

    s = [state[:, p * LANES:(p + 1) * LANES] for p in range(npairs)]
    for c in range(nck):
        rows = slice(c * L, (c + 1) * L)
        os = []
        for p in range(npairs):
            pl_ = slice(p * LANES, (p + 1) * LANES)
            sb = s[p].astype(BF16)
            os.append(lax.dot_general(rhat_ref[0, rows, pl_], block_diag(sb), nt, preferred_element_type=F32)
                      + ohat_ref[0, rows, pl_])
            s[p] = (s[p] * pt_ref[0, c, :, pl_]
                    + jnp.dot(sb, block_diag(m_ref[0, c, :, pl_]), preferred_element_type=F32)
                    + c_ref[0, c, :, pl_])
        ds = [o - m for o, m in zip(os, [head_sum(o) * (1.0 / N) for o in os])]
        variances = [head_sum(d * d) * (1.0 / N) for d in ds]
        for p, (d, var) in enumerate(zip(ds, variances)):
            pl_ = slice(p * LANES, (p + 1) * LANES)
            o_ref[0, rows, pl_] = (d * lax.rsqrt(var + GN_EPS) * lnw_ref[:, pl_] + lnb_ref[:, pl_]
                                   + bonus_ref[0, rows, pl_])
    for p in range(npairs):
        state[:, p * LANES:(p + 1) * LANES] = s[p]

    @pl.when(pl.program_id(1) == pl.num_programs(1) - 1)
    def _():
        sT_ref[0] = state[...]


def _rwkv_state(rhat, ohat, bonus, m, c, pt, s0, p):
    B, T, D = ohat.shape
    L = min(CHUNK, T)
    ts = min(STATE_TILE, T)
    nck = ts // L
    N = RW_HEAD
    H = D // N
    tile = pl.BlockSpec((1, ts, D), lambda b, t: (b, t, 0))
    sq = pl.BlockSpec((1, nck, N, D), lambda b, t: (b, t, 0, 0))
    st = pl.BlockSpec((1, N, D), lambda b, t: (b, 0, 0))
    s0_vk = jnp.transpose(s0, (0, 2, 1, 3)).reshape(B, N, D)
    o, sT = pl.pallas_call(
        functools.partial(_rwkv_state_kernel, L),
        grid=(B, T // ts),
        in_specs=[tile, tile, tile, sq, sq, pl.BlockSpec((1, nck, 1, D), lambda b, t: (b, t, 0, 0)), st,
                  _const_spec((1, D)), _const_spec((1, D))],
        out_specs=[tile, st],
        out_shape=[jax.ShapeDtypeStruct((B, T, D), F32), jax.ShapeDtypeStruct((B, N, D), F32)],
        scratch_shapes=[pltpu.VMEM((N, D), F32)],
        compiler_params=_params(("parallel", "arbitrary")),
        name="rwkv_state",
    )(rhat, ohat, bonus, m, c, pt, s0_vk, p['ln_w'], p['ln_b'])
    return o, jnp.transpose(sT.reshape(B, N, H, N), (0, 2, 1, 3))


def _rwkv_scan(rt, at, bt, kt, v, pt, s0, p):
    rhat, ohat, bonus, m, c = _rwkv_maps(rt, at, bt, kt, v, pt, p['r_k'])
    return _rwkv_state(rhat, ohat, bonus, m, c, pt, s0, p)


def _post_kernel(x_ref, a_ref, b_ref, wo_ref, gmix_ref, gpre_ref, wgu_ref, wd_ref, gpost_ref, o_ref):
    f = wd_ref.shape[0]
    tm = x_ref.shape[0]
    rows = min(POST_ROWS, tm)
    groups = [slice(i * rows, (i + 1) * rows) for i in range(tm // rows)]
    ys = [_mm(a_ref[r, :] * b_ref[r, :], wo_ref[...]) for r in groups]
    xs = [x_ref[r, :] + _rms(y, gmix_ref[...]) for r, y in zip(groups, ys)]
    gus = [_mm(_rms(x, gpre_ref[...]), wgu_ref[...]) for x in xs]
    acts = [gu[:, :f] * jax.nn.sigmoid(gu[:, :f]) * gu[:, f:] for gu in gus]
    downs = [_mm(act, wd_ref[...]) for act in acts]
    for r, x, down in zip(groups, xs, downs):
        o_ref[r, :] = x + _rms(down, gpost_ref[...])


def _post(x, a, b, wo, lp):
    R, D = x.shape
    tm = min(TOKEN_TILE, R)
    tile = pl.BlockSpec((tm, D), lambda t: (t, 0))
    consts = (wo, lp['g_mix'], lp['g_ffn_pre'], lp['ffn_w_gu'], lp['ffn_w_down'], lp['g_ffn_post'])
    return pl.pallas_call(
        _post_kernel,
        grid=(R // tm,),
        in_specs=[tile, tile, tile] + [_const_spec(c.shape) for c in consts],
        out_specs=tile,
        out_shape=jax.ShapeDtypeStruct((R, D), F32),
        compiler_params=_params(("parallel",)),
        name="post",
    )(x, a, b, *consts)


def _mlstm_pre_kernel(L, nq, x_ref, g0_ref, win_ref, wgc_ref, wgr_ref, bc_ref, br_ref,
                      q_ref, k_ref, v_ref, og_ref, gc_ref, gr_ref):
    nv = v_ref.shape[1]
    dqk = nq // ML_HEADS
    tm = x_ref.shape[0]
    rows = min(POST_ROWS, tm)
    groups = [slice(i * rows, (i + 1) * rows) for i in range(tm // rows)]
    hs = [_rms(x_ref[r, :], g0_ref[...]).astype(BF16) for r in groups]
    zs = [jnp.dot(hg, win_ref[...], preferred_element_type=F32) for hg in hs]
    for r, z in zip(groups, zs):
        q_ref[r, :] = z[:, :nq] * (dqk ** -0.5)
        k_ref[r, :] = z[:, nq:2 * nq]
        v_ref[r, :] = z[:, 2 * nq:2 * nq + nv]
        og_ref[r, :] = jax.nn.sigmoid(z[:, 2 * nq + nv:])
    h = jnp.concatenate(hs, axis=0) if len(hs) > 1 else hs[0]

    def cap(t):
        return GATE_CAP * jnp.tanh(t / GATE_CAP)

    gc = cap(jnp.dot(h, wgc_ref[...], preferred_element_type=F32) + bc_ref[...])
    lane = lax.broadcasted_iota(jnp.int32, gc.shape, 1)
    gc_ref[...] = jnp.where(lane < ML_HEADS, gc, jax.nn.log_sigmoid(gc))
    gr = cap(lax.dot_general(wgr_ref[...], h, (((1,), (1,)), ((), ())), preferred_element_type=F32)
             + br_ref[...])
    rowi = lax.broadcasted_iota(jnp.int32, gr.shape, 0)
    gr = jnp.where(rowi < ML_HEADS, gr, jax.nn.log_sigmoid(gr))
    for c in range(gr_ref.shape[0]):
        gr_ref[c] = gr[:, c * L:(c + 1) * L]


def _mlstm_pre(x, L, p):
    R, D = x.shape
    tm = min(TOKEN_TILE, R)
    nq = ML_HEADS * (D // (2 * ML_HEADS))
    nv = D
    consts = [p['g0'], p['w_in'], p['w_gc'], p['w_gr'], p['b_c'], p['b_r']]
    row = lambda w: pl.BlockSpec((tm, w), lambda t: (t, 0))
    return pl.pallas_call(
        functools.partial(_mlstm_pre_kernel, L, nq),
        grid=(R // tm,),
        in_specs=[row(D)] + [_const_spec(c.shape) for c in consts],
        out_specs=[row(nq), row(nq), row(nv), row(nv), row(LANES),
                   pl.BlockSpec((tm // L, GATE_ROWS, L), lambda t: (t, 0, 0))],
        out_shape=[jax.ShapeDtypeStruct((R, nq), F32), jax.ShapeDtypeStruct((R, nq), F32),
                   jax.ShapeDtypeStruct((R, nv), F32), jax.ShapeDtypeStruct((R, nv), F32),
                   jax.ShapeDtypeStruct((R, LANES), F32),
                   jax.ShapeDtypeStruct((R // L, GATE_ROWS, L), F32)],
        compiler_params=_params(("parallel",)),
        name="mlstm_pre",
    )(x, *consts)


def _mlstm_scan_kernel(L, q_ref, k_ref, v_ref, gc_ref, gr_ref, c0_ref, n0_ref, m0_ref, hnw_ref,
                       h_ref, cT_ref, nT_ref, mT_ref, c_st, n_st, m_st):
    H = ML_HEADS
    nck = q_ref.shape[0] // L
    dqk = q_ref.shape[1] // H
    dv = v_ref.shape[1] // H

    @pl.when(pl.program_id(1) == 0)
    def _():
        c_st[...] = c0_ref[0]
        n_st[...] = n0_ref[0]
        m_st[...] = m0_ref[0]

    row = lax.broadcasted_iota(jnp.int32, (L, L), 0)
    col = lax.broadcasted_iota(jnp.int32, (L, L), 1)
    causal = row >= col
    tril = causal.astype(BF16)
    triu = (row <= col).astype(BF16)
    nt = (((1,), (1,)), ((), ()))
    tn = (((0,), (0,)), ((), ()))

    gcs = [gc_ref[c * L:(c + 1) * L, :] for c in range(nck)]
    grs = [gr_ref[c] for c in range(nck)]
    bcs = [_dot_exact_lhs(tril, gc) for gc in gcs]
    brs = []
    for gr in grs:
        hi, mid, lo = _split3(gr)
        brs.append(jnp.dot(hi, triu, preferred_element_type=F32) + jnp.dot(mid, triu, preferred_element_type=F32)
                   + jnp.dot(lo, triu, preferred_element_type=F32))

    where = [(c, hd) for c in range(nck) for hd in range(H)]
    rows_of = [slice(c * L, (c + 1) * L) for c, _ in where]
    qs = [q_ref[rows, hd * dqk:(hd + 1) * dqk] for rows, (_, hd) in zip(rows_of, where)]
    ks = [k_ref[rows, hd * dqk:(hd + 1) * dqk] for rows, (_, hd) in zip(rows_of, where)]
    vbs = [v_ref[rows, hd * dv:(hd + 1) * dv].astype(BF16) for rows, (_, hd) in zip(rows_of, where)]
    li_cs = [jnp.broadcast_to(gcs[c][:, hd:hd + 1], (L, dqk)) for c, hd in where]
    b_cs = [jnp.broadcast_to(bcs[c][:, H + hd:H + hd + 1], (L, dqk)) for c, hd in where]
    dmats = [jnp.where(causal, b_c[:, :L] - brs[c][H + hd:H + hd + 1, :] + grs[c][hd:hd + 1, :], -jnp.inf)
             for (c, hd), b_c in zip(where, b_cs)]
    intras = [jnp.broadcast_to(jnp.max(dmat, axis=-1, keepdims=True), (L, dqk)) for dmat in dmats]
    gs = [intra - b_c for intra, b_c in zip(intras, b_cs)]
    g_ends = [g[L - 1:L, :] for g in gs]
    g_lasts = [g_end[:, :1] for g_end in g_ends]
    b_lasts = [b_c[L - 1:L, :1] for b_c in b_cs]
    es = [jnp.exp(dmat - intra[:, :L]) for dmat, intra in zip(dmats, intras)]
    kws = [k * jnp.exp(li_c - b_c - g_end) for k, li_c, b_c, g_end in zip(ks, li_cs, b_cs, g_ends)]
    qbs = [q.astype(BF16) for q in qs]
    qks = [lax.dot_general(qb, k.astype(BF16), nt, preferred_element_type=F32) * e
           for qb, k, e in zip(qbs, ks, es)]
    num_i = [_mm(qk, vb) for qk, vb in zip(qks, vbs)]
    kvs = [lax.dot_general(kw.astype(BF16), vb, tn, preferred_element_type=F32) for kw, vb in zip(kws, vbs)]
    ksums = [jnp.sum(kw, axis=0, keepdims=True) for kw in kws]
    qk_sums = [jnp.sum(qk, axis=-1, keepdims=True) for qk in qks]

    c_cur = [c_st[hd] for hd in range(H)]
    n_cur = [n_st[hd] for hd in range(H)]
    m_cur = [m_st[hd] for hd in range(H)]
    c_prevs, n_prevs, m_prevs = [], [], []
    for (c, hd), g_last, b_last, kv, ksum in zip(where, g_lasts, b_lasts, kvs, ksums):
        c_prevs.append(c_cur[hd].astype(BF16))
        n_prevs.append(n_cur[hd])
        m_prevs.append(m_cur[hd])
        mx = jnp.maximum(m_cur[hd], g_last)
        e_old = jnp.exp(m_cur[hd] - mx)
        e_new = jnp.exp(g_last - mx)
        c_cur[hd] = e_old * c_cur[hd] + e_new * kv
        n_cur[hd] = e_old * n_cur[hd] + e_new * ksum
        m_cur[hd] = b_last + mx
    for hd in range(H):
        c_st[hd] = c_cur[hd]
        n_st[hd] = n_cur[hd]
        m_st[hd] = m_cur[hd]

    qcs = [_mm(qb, cp) for qb, cp in zip(qbs, c_prevs)]
    qn_sums = [jnp.sum(q * n_prev, axis=-1, keepdims=True) for q, n_prev in zip(qs, n_prevs)]
    mxs = [jnp.maximum(m_prev, g) for g, m_prev in zip(gs, m_prevs)]
    f_news = [jnp.exp(g - mx) for g, mx in zip(gs, mxs)]
    f_olds = [jnp.exp(m_prev - mx) for mx, m_prev in zip(mxs, m_prevs)]
    floors = [jnp.exp(-(b_c + mx)) for b_c, mx in zip(b_cs, mxs)]
    invs = [1.0 / jnp.maximum(jnp.abs(f_new * qk_sum + f_old * qn_sum), floor)
            for f_new, f_old, qk_sum, qn_sum, floor in zip(f_news, f_olds, qk_sums, qn_sums, floors)]
    wide = lambda a: jnp.concatenate([a] * (dv // dqk), axis=1)
    hs = [wide(f_new * inv) * ni + wide(f_old * inv) * qc
          for f_new, f_old, inv, ni, qc in zip(f_news, f_olds, invs, num_i, qcs)]
    h_ms = [jnp.mean(h * h, axis=-1, keepdims=True) for h in hs]
    for rows, (_, hd), h, h_m in zip(rows_of, where, hs, h_ms):
        h_ref[rows, hd * dv:(hd + 1) * dv] = h * lax.rsqrt(h_m + EPS) * hnw_ref[:, hd * dv:(hd + 1) * dv]

    @pl.when(pl.program_id(1) == pl.num_programs(1) - 1)
    def _():
        cT_ref[0] = c_st[...]
        nT_ref[0] = n_st[...]
        mT_ref[0] = m_st[...]


def _mlstm_scan(q, k, v, gc, gr, c0, n0, m0, hn_w, B, T):
    L = min(CHUNK, T)
    ts = min(MLSTM_TILE, T)
    nc = T // ts
    H = ML_HEADS
    nq, nv = q.shape[1], v.shape[1]
    dqk, dv = nq // H, nv // H
    row = lambda w: pl.BlockSpec((ts, w), lambda b, c: (b * nc + c, 0))
    st = lambda *s: pl.BlockSpec((1,) + s, lambda b, c: (b,) + (0,) * len(s))
    return pl.pallas_call(
        functools.partial(_mlstm_scan_kernel, L),
        grid=(B, nc),
        in_specs=[row(nq), row(nq), row(nv), row(LANES),
                  pl.BlockSpec((ts // L, GATE_ROWS, L), lambda b, c: (b * nc + c, 0, 0)),
                  st(H, dqk, dv), st(H, 1, dqk), st(H, 1, 1), _const_spec(hn_w.shape)],
        out_specs=[row(nv), st(H, dqk, dv), st(H, 1, dqk), st(H, 1, 1)],
        out_shape=[jax.ShapeDtypeStruct((B * T, nv), F32), jax.ShapeDtypeStruct((B, H, dqk, dv), F32),
                   jax.ShapeDtypeStruct((B, H, 1, dqk), F32), jax.ShapeDtypeStruct((B, H, 1, 1), F32)],
        scratch_shapes=[pltpu.VMEM((H, dqk, dv), F32), pltpu.VMEM((H, 1, dqk), F32),
                        pltpu.VMEM((H, 1, 1), F32)],
        compiler_params=_params(("parallel", "arbitrary")),
        name="mlstm_scan",
    )(q, k, v, gc, gr, c0, n0.reshape(B, H, 1, dqk), m0.reshape(B, H, 1, 1), hn_w)


def _prep_weights(w):
    D = w['norm_g'].shape[-1]
    H = D // RW_HEAD
    row = lambda a: a.reshape(1, -1).astype(F32)
    head_of_lane = jnp.arange(D) // RW_HEAD
    hsum = (head_of_lane[:, None] == jnp.arange(LANES)[None, :]).astype(BF16)
    hexp = hsum.T
    layers = []
    for i in range(DEPTH):
        j = i // N_MIXERS
        ng = w['norm_g'][i]
        lp = {'g0': row(ng[0]), 'g_mix': row(ng[1]), 'g_ffn_pre': row(ng[2]), 'g_ffn_post': row(ng[3]),
              'ffn_w_gu': w['ffn_w_gu'][i].astype(BF16), 'ffn_w_down': w['ffn_w_down'][i].astype(BF16)}
        if i % N_MIXERS == 0:
            lp.update({
                'mu': jnp.pad(w['rw_mu'][j], ((0, 2), (0, 0))),
                'w_rkv': w['rw_w_rkv'][j].astype(BF16),
                'w0': row(w['rw_w0'][j]), 'w1': w['rw_w1'][j].astype(BF16), 'w2': w['rw_w2'][j].astype(BF16),
                'a0': row(w['rw_a0'][j]), 'a1': w['rw_a1'][j].astype(BF16), 'a2': w['rw_a2'][j].astype(BF16),
                'g1': w['rw_g1'][j].astype(BF16), 'g2': w['rw_g2'][j].astype(BF16),
                'k_k': row(w['rw_k_k'][j]), 'k_a': row(w['rw_k_a'][j]), 'r_k': row(w['rw_r_k'][j]),
                'ln_w': row(w['rw_ln_w'][j]), 'ln_b': row(w['rw_ln_b'][j]),
                'w_o': w['rw_w_o'][j].astype(BF16), 'hsum': hsum, 'hexp': hexp,
            })
            if j > 0:
                lp.update({'v0': row(w['rw_v0'][j - 1]), 'v1': w['rw_v1'][j - 1].astype(BF16),
                           'v2': w['rw_v2'][j - 1].astype(BF16)})
        else:
            w_in = w['ml_w_in'][j]
            n_main = w_in.shape[1] - 2 * ML_HEADS
            w_g = w_in[:, n_main:]
            bias = w['ml_b_gates'][j].astype(F32)
            lp.update({
                'w_in': w_in[:, :n_main].astype(BF16),
                'w_gc': jnp.pad(w_g, ((0, 0), (0, LANES - 2 * ML_HEADS))).astype(BF16),
                'w_gr': jnp.pad(w_g.T, ((0, GATE_ROWS - 2 * ML_HEADS), (0, 0))).astype(BF16),
                'b_c': jnp.pad(bias, (0, LANES - 2 * ML_HEADS)).reshape(1, LANES),
                'b_r': jnp.pad(bias, (0, GATE_ROWS - 2 * ML_HEADS)).reshape(GATE_ROWS, 1),
                'hn_w': row(w['ml_hn_w'][j]), 'w_out': w['ml_w_out'][j].astype(BF16),
            })
        layers.append(lp)
    return layers


def _trunk(x, rw_shift, rw_S, ml_C, ml_n, ml_m, layers):
    B, T, D = x.shape
    L = min(CHUNK, T)
    flat = lambda a: a.reshape(B * T, a.shape[-1])
    v_first = None
    shifts, Ss, Cs, ns, ms = [], [], [], [], []
    xf = flat(x)
    for i, lp in enumerate(layers):
        j = i // N_MIXERS
        if i % N_MIXERS == 0:
            rt, at, bt, kt, v, g, pt, sh = _rwkv_pre(xf.reshape(B, T, D), rw_shift[j], lp, j, v_first)
            if j == 0:
                v_first = v
            o, S = _rwkv_scan(rt, at, bt, kt, v, pt, rw_S[j], lp)
            shifts.append(sh)
            Ss.append(S)
            xf = _post(xf, flat(o), flat(g), lp['w_o'], lp)
        else:
            q, k, v, og, gc, gr = _mlstm_pre(xf, L, lp)
            hn, C, n, m = _mlstm_scan(q, k, v, gc, gr, ml_C[j], ml_n[j], ml_m[j], lp['hn_w'], B, T)
            Cs.append(C)
            ns.append(n.reshape(B, ML_HEADS, -1))
            ms.append(m.reshape(B, ML_HEADS))
            xf = _post(xf, og, hn, lp['w_out'], lp)
    return xf.reshape(B, T, D), jnp.stack(shifts), jnp.stack(Ss), jnp.stack(Cs), jnp.stack(ns), jnp.stack(ms)


def kernel(x_prompt, x_sample, state_rwkv_shift, state_rwkv_S, state_mlstm_C, state_mlstm_n, state_mlstm_m,
           norm_g, rw_mu, rw_w_rkv, rw_w0, rw_w1, rw_w2, rw_a0, rw_a1, rw_a2, rw_v0, rw_v1, rw_v2,
           rw_g1, rw_g2, rw_k_k, rw_k_a, rw_r_k, rw_ln_w, rw_ln_b, rw_w_o,
           ml_w_in, ml_b_gates, ml_hn_w, ml_w_out, ffn_w_gu, ffn_w_down):
    w = {
        'norm_g': norm_g, 'rw_mu': rw_mu, 'rw_w_rkv': rw_w_rkv, 'rw_w0': rw_w0, 'rw_w1': rw_w1,
        'rw_w2': rw_w2, 'rw_a0': rw_a0, 'rw_a1': rw_a1, 'rw_a2': rw_a2, 'rw_v0': rw_v0, 'rw_v1': rw_v1,
        'rw_v2': rw_v2, 'rw_g1': rw_g1, 'rw_g2': rw_g2, 'rw_k_k': rw_k_k, 'rw_k_a': rw_k_a,
        'rw_r_k': rw_r_k, 'rw_ln_w': rw_ln_w, 'rw_ln_b': rw_ln_b, 'rw_w_o': rw_w_o,
        'ml_w_in': ml_w_in, 'ml_b_gates': ml_b_gates, 'ml_hn_w': ml_hn_w, 'ml_w_out': ml_w_out,
        'ffn_w_gu': ffn_w_gu, 'ffn_w_down': ffn_w_down,
    }
    layers = _prep_weights(w)
    Bp = x_prompt.shape[0]
    zeros = lambda a: jnp.zeros((a.shape[0], Bp) + a.shape[2:], F32)
    out_p = _trunk(x_prompt, zeros(state_rwkv_shift), zeros(state_rwkv_S), zeros(state_mlstm_C),
                   zeros(state_mlstm_n), zeros(state_mlstm_m), layers)
    out_s = _trunk(x_sample, state_rwkv_shift, state_rwkv_S, state_mlstm_C, state_mlstm_n, state_mlstm_m,
                   layers)
    return (out_p[0], out_s[0]) + tuple(out_p[1:]) + tuple(out_s[1:])
```

```python
import functools
import math

import jax
import jax.numpy as jnp
from jax import lax
from jax.experimental import pallas as pl
from jax.experimental.pallas import tpu as pltpu

F32 = jnp.float32
BF16 = jnp.bfloat16

DEPTH = 4
N_MIXERS = 2
CHUNK = 64
RW_HEAD = 64
ML_HEADS = 4
GN_EPS = 64e-5
GATE_CAP = 15.0
EPS = 1e-6

LANES = 128
VMEM_LIMIT = 56 * 1024 * 1024
TOKEN_TILE = 512
POST_ROWS = 256
PRE_ROWS = 128
RWKV_PRE_TILE = 256
MAPS_TILE = 512
MAPS_LANES = 512
STATE_TILE = 512
MLSTM_TILE = 256
GATE_ROWS = 16


def _const_spec(shape):
    nd = len(shape)
    return pl.BlockSpec(shape, lambda *_: (0,) * nd, pipeline_mode=pl.Buffered(1))


def _params(sem, vmem=VMEM_LIMIT):
    return pltpu.CompilerParams(dimension_semantics=sem, vmem_limit_bytes=vmem)


def _rms(x, g):
    return x * lax.rsqrt(jnp.mean(x * x, axis=-1, keepdims=True) + EPS) * g


def _mm(a, w):
    return jnp.dot(a.astype(BF16), w, preferred_element_type=F32)


def _split2(x):
    hi = x.astype(BF16)
    lo = (x - hi.astype(F32)).astype(BF16)
    return hi, lo


def _split3(x):
    hi = x.astype(BF16)
    r1 = x - hi.astype(F32)
    mid = r1.astype(BF16)
    lo = (r1 - mid.astype(F32)).astype(BF16)
    return hi, mid, lo


def _dot_exact_rhs(x, w01):
    hi, lo = _split2(x)
    return (jnp.dot(hi, w01, preferred_element_type=F32)
            + jnp.dot(lo, w01, preferred_element_type=F32))


def _dot_exact_lhs(w01, x):
    hi, mid, lo = _split3(x)
    return (jnp.dot(w01, hi, preferred_element_type=F32)
            + jnp.dot(w01, mid, preferred_element_type=F32)
            + jnp.dot(w01, lo, preferred_element_type=F32))


def _rwkv_pre_kernel(has_vfirst, nct, *refs):
    (x_ref, shift0_ref, g0_ref, mu_ref, wrkv_ref, w0_ref, w1_ref, w2_ref, a0_ref, a1_ref, a2_ref,
     g1_ref, g2_ref, kk_ref, ka_ref, hsum_ref, hexp_ref, tri_ref, sel_ref) = refs[:19]
    refs = refs[19:]
    if has_vfirst:
        v0_ref, v1_ref, v2_ref, vfirst_ref = refs[:4]
        refs = refs[4:]
    rt_ref, at_ref, bt_ref, kt_ref, v_ref, g_ref, pt_ref, shift_ref, carry = refs

    tm = x_ref.shape[1]

    @pl.when(pl.program_id(1) == 0)
    def _():
        carry[...] = shift0_ref[0]

    h = _rms(x_ref[0], g0_ref[...])
    row = lax.broadcasted_iota(jnp.int32, h.shape, 0)
    prev = jnp.where(row == 0, carry[...], pltpu.roll(h, 1, axis=0))
    last = h[tm - 1:tm, :]
    carry[...] = last
    shift_ref[0] = last

    rows = min(PRE_ROWS, tm)
    groups = [slice(i * rows, (i + 1) * rows) for i in range(tm // rows)]
    hs = [h[r, :] for r in groups]
    dxs = [prev[r, :] - hg for r, hg in zip(groups, hs)]
    mixed = [[(hg + dx * mu_ref[c:c + 1, :]).astype(BF16) for c in range(6)] for hg, dx in zip(hs, dxs)]
    dot = lambda a, w: jnp.dot(a, w, preferred_element_type=F32)
    rs = [dot(m[0], wrkv_ref[0]) for m in mixed]
    ks = [dot(m[2], wrkv_ref[1]) for m in mixed]
    vs = [dot(m[3], wrkv_ref[2]) for m in mixed]
    w_lo = [jnp.tanh(dot(m[1], w1_ref[...])) for m in mixed]
    a_lo = [dot(m[4], a1_ref[...]) for m in mixed]
    g_lo = [jax.nn.sigmoid(dot(m[5], g1_ref[...])) for m in mixed]
    lws = [(-math.exp(-0.5)) * jax.nn.sigmoid(w0_ref[...] + _mm(t, w2_ref[...])) for t in w_lo]
    as_ = [jax.nn.sigmoid(a0_ref[...] + _mm(t, a2_ref[...])) for t in a_lo]
    gs = [_mm(t, g2_ref[...]) for t in g_lo]
    if has_vfirst:
        v_lo = [dot(m[3], v1_ref[...]) for m in mixed]
        mixes = [jax.nn.sigmoid(v0_ref[...] + _mm(t, v2_ref[...])) for t in v_lo]
        vs = [v + (vfirst_ref[0, r, :] - v) * mix for r, v, mix in zip(groups, vs, mixes)]

    kkrs = [k * kk_ref[...] for k in ks]
    ssqs = [_mm(kkr * kkr, hsum_ref[...]) for kkr in kkrs]
    invs = [lax.rsqrt(jnp.maximum(ssq, 1e-24)) for ssq in ssqs]
    kks = [kkr * _dot_exact_rhs(inv, hexp_ref[...]) for kkr, inv in zip(kkrs, invs)]
    kmods = [k * (1.0 + (a - 1.0) * ka_ref[...]) for k, a in zip(ks, as_)]

    splits = [_split2(lw) for lw in lws]
    tri = tri_ref[:rows, :rows]
    bs = [dot(tri, hi) + dot(tri, lo) for hi, lo in splits]
    pt_log = sum(dot(sel_ref[:, r], hi) + dot(sel_ref[:, r], lo) for r, (hi, lo) in zip(groups, splits))
    for r, rr, k_mod, kk, a, v, g, lw, b in zip(groups, rs, kmods, kks, as_, vs, gs, lws, bs):
        enb = jnp.exp(-b)
        rt_ref[0, r, :] = rr * jnp.exp(b)
        at_ref[0, r, :] = -kk * jnp.exp(b - lw)
        bt_ref[0, r, :] = kk * a * enb
        kt_ref[0, r, :] = k_mod * enb
        v_ref[0, r, :] = v
        g_ref[0, r, :] = g
    pt_ref[0, 0] = jnp.exp(pt_log)[:nct]


def _rwkv_pre(x, shift0, p, j, v_first):
    B, T, D = x.shape
    L = min(CHUNK, T)
    tm = min(RWKV_PRE_TILE, T)
    nct = tm // L
    has_vfirst = v_first is not None
    row = jnp.arange(tm)
    tri = ((row[:, None] >= row[None, :]) & (row[:, None] // L == row[None, :] // L)).astype(BF16)
    sel = (jnp.arange(GATE_ROWS)[:, None] == row[None, :] // L).astype(BF16)

    tile = pl.BlockSpec((1, tm, D), lambda b, t: (b, t, 0))
    consts = [p['g0'], p['mu'], p['w_rkv'], p['w0'], p['w1'], p['w2'], p['a0'], p['a1'], p['a2'],
              p['g1'], p['g2'], p['k_k'], p['k_a'], p['hsum'], p['hexp'], tri, sel]
    args = [x, shift0.reshape(B, 1, D)] + consts
    specs = [tile, pl.BlockSpec((1, 1, D), lambda b, t: (b, 0, 0))] + [_const_spec(c.shape) for c in consts]
    if has_vfirst:
        vc = [p['v0'], p['v1'], p['v2']]
        args += vc + [v_first]
        specs += [_const_spec(c.shape) for c in vc] + [tile]

    big = jax.ShapeDtypeStruct((B, T, D), F32)
    outs = pl.pallas_call(
        functools.partial(_rwkv_pre_kernel, has_vfirst, nct),
        grid=(B, T // tm),
        in_specs=specs,
        out_specs=[tile] * 6 + [pl.BlockSpec((1, 1, nct, D), lambda b, t: (b, t, 0, 0)),
                                pl.BlockSpec((1, 1, D), lambda b, t: (b, 0, 0))],
        out_shape=[big] * 6 + [jax.ShapeDtypeStruct((B, T // tm, nct, D), F32),
                               jax.ShapeDtypeStruct((B, 1, D), F32)],
        scratch_shapes=[pltpu.VMEM((1, D), F32)],
        compiler_params=_params(("parallel", "arbitrary")),
        name="rwkv_pre",
    )(*args)
    rt, at, bt, kt, v, g, pt, shift = outs
    return rt, at, bt, kt, v, g, pt.reshape(B, T // L, 1, D), shift.reshape(B, D)


def _rwkv_chunk_maps(L, items):
    N = RW_HEAD
    nt = (((1,), (1,)), ((), ()))
    tn = (((0,), (0,)), ((), ()))
    head0_f = lax.broadcasted_iota(jnp.int32, (1, 2 * N), 1) < N
    head0_t = lax.broadcasted_iota(jnp.int32, (1, 2 * L), 1) < L
    zero = jnp.zeros((), BF16)

    def split(z, first):
        return jnp.concatenate([jnp.where(first, z, zero), jnp.where(first, zero, z)], axis=0)

    row = lax.broadcasted_iota(jnp.int32, (L, 4 * L), 0)
    col = lax.broadcasted_iota(jnp.int32, (L, 4 * L), 1) % L
    strict = row > col
    incl = row >= col
    eye = jnp.where(lax.broadcasted_iota(jnp.int32, (L, 2 * L), 0)
                    == lax.broadcasted_iota(jnp.int32, (L, 2 * L), 1) % L, 1.0, 0.0)

    ats = [it[0].astype(BF16) for it in items]
    vbs = [it[4].astype(BF16) for it in items]
    v01s = [split(vb, head0_f) for vb in vbs]
    gs = [lax.dot_general(jnp.concatenate([at, it[1].astype(BF16)], axis=0),
                          jnp.concatenate([split(it[2].astype(BF16), head0_f),
                                           split(it[3].astype(BF16), head0_f)], axis=0),
                          nt, preferred_element_type=F32)
          for it, at in zip(items, ats)]
    tops = [jnp.where(strict, g[:L], 0.0) for g in gs]
    bots = [jnp.where(incl, g[L:], 0.0).astype(BF16) for g in gs]
    avs = [_mm(top[:, 2 * L:], v01) for top, v01 in zip(tops, v01s)]
    n_pows = [top[:, :2 * L].astype(BF16) for top in tops]
    ts = [eye + top[:, :2 * L] for top in tops]
    n_pows = [jnp.dot(n, split(n, head0_t), preferred_element_type=F32).astype(BF16) for n in n_pows]
    span = 2
    while True:
        span *= 2
        if span >= L:
            ts = [t + jnp.dot(n, split(t.astype(BF16), head0_t), preferred_element_type=F32)
                  for t, n in zip(ts, n_pows)]
            break
        res = [jnp.dot(n, jnp.concatenate([split(t.astype(BF16), head0_t), split(n, head0_t)], axis=1),
                       preferred_element_type=F32) for t, n in zip(ts, n_pows)]
        ts = [t + r[:, :2 * L] for t, r in zip(ts, res)]
        n_pows = [r[:, 2 * L:].astype(BF16) for r in res]

    wus = [_mm(t, jnp.concatenate([split(at, head0_f), split(av.astype(BF16), head0_f)], axis=1)).astype(BF16)
           for t, at, av in zip(ts, ats, avs)]
    zeros2 = jnp.zeros((2 * L, 2 * N), BF16)
    ros = [jnp.dot(bot, jnp.concatenate(
               [jnp.concatenate([split(wu[:, :2 * N], head0_f), split(wu[:, 2 * N:], head0_f)], axis=1),
                jnp.concatenate([zeros2, v01], axis=1)], axis=0), preferred_element_type=F32)
           for bot, wu, v01 in zip(bots, wus, v01s)]
    zs = [jnp.concatenate([wu, jnp.concatenate([zeros2[:L], vb], axis=1)], axis=0)
          for wu, vb in zip(wus, vbs)]
    mcs = [lax.dot_general(z, jnp.concatenate([it[2] * it[5], it[3] * it[5]], axis=0).astype(BF16), tn,
                           preferred_element_type=F32)
           for it, z in zip(items, zs)]
    return [(it[1] + ro[:, :2 * N], ro[:, 2 * N:],
             jnp.where(head0_f, mc[:N], mc[N:2 * N]), jnp.where(head0_f, mc[2 * N:3 * N], mc[3 * N:]))
            for it, ro, mc in zip(items, ros, mcs)]


def _rwkv_maps_kernel(L, rt_ref, at_ref, bt_ref, kt_ref, v_ref, pt_ref, rk_ref,
                      rhat_ref, ohat_ref, bonus_ref, m_ref, c_ref):
    N = RW_HEAD
    nck = rt_ref.shape[1] // L
    where = [(slice(c * L, (c + 1) * L), slice(p * LANES, (p + 1) * LANES), c)
             for c in range(nck) for p in range(rt_ref.shape[2] // LANES)]
    items = [(at_ref[0, rows, sl], rt_ref[0, rows, sl], bt_ref[0, rows, sl], kt_ref[0, rows, sl],
              v_ref[0, rows, sl], pt_ref[0, c, :, sl]) for rows, sl, c in where]
    maps = _rwkv_chunk_maps(L, items)
    head0 = lax.broadcasted_iota(jnp.int32, (1, LANES), 1) < N
    for (rows, sl, c), it, (rhat, ohat, m, cc) in zip(where, items, maps):
        _, rt, _, kt, v, _ = it
        rk = rt * kt * rk_ref[:, sl]
        total = jnp.sum(rk, axis=-1, keepdims=True)
        part = jnp.sum(jnp.where(head0, rk, 0.0), axis=-1, keepdims=True)
        rhat_ref[0, rows, sl] = rhat.astype(BF16)
        ohat_ref[0, rows, sl] = ohat
        bonus_ref[0, rows, sl] = jnp.where(head0, part, total - part) * v
        m_ref[0, c, :, sl] = m.astype(BF16)
        c_ref[0, c, :, sl] = cc


def _rwkv_maps(rt, at, bt, kt, v, pt, r_k):
    B, T, D = rt.shape
    L = min(CHUNK, T)
    ts = min(MAPS_TILE, T)
    nck = ts // L
    N = RW_HEAD
    tile = pl.BlockSpec((1, ts, MAPS_LANES), lambda b, h, c: (b, c, h))
    sq = pl.BlockSpec((1, nck, N, MAPS_LANES), lambda b, h, c: (b, c, 0, h))
    big = lambda dt: jax.ShapeDtypeStruct((B, T, D), dt)
    return pl.pallas_call(
        functools.partial(_rwkv_maps_kernel, L),
        grid=(B, D // MAPS_LANES, T // ts),
        in_specs=[tile] * 5 + [pl.BlockSpec((1, nck, 1, MAPS_LANES), lambda b, h, c: (b, c, 0, h)),
                               pl.BlockSpec((1, MAPS_LANES), lambda b, h, c: (0, h))],
        out_specs=[tile, tile, tile, sq, sq],
        out_shape=[big(BF16), big(F32), big(F32),
                   jax.ShapeDtypeStruct((B, T // L, N, D), BF16), jax.ShapeDtypeStruct((B, T // L, N, D), F32)],
        compiler_params=_params(("parallel", "parallel", "parallel")),
        name="rwkv_maps",
    )(rt, at, bt, kt, v, pt, r_k)


def _rwkv_state_kernel(L, rhat_ref, ohat_ref, bonus_ref, m_ref, c_ref, pt_ref, s0_ref, lnw_ref, lnb_ref,
                       o_ref, sT_ref, state):
    N = RW_HEAD
    nck = rhat_ref.shape[1] // L
    npairs = rhat_ref.shape[2] // LANES

    @pl.when(pl.program_id(1) == 0)
    def _():
        state[...] = s0_ref[0]

    nt = (((1,), (1,)), ((), ()))
    lane = lax.broadcasted_iota(jnp.int32, (1, LANES), 1)
    first = lane < N
    zero = jnp.zeros((), BF16)

    def block_diag(a):
        return jnp.concatenate([jnp.where(first, a, zero), jnp.where(first, zero, a)], axis=0)

    def head_sum(a):
        total = jnp.sum(a, axis=-1, keepdims=True)
        part = jnp.sum(jnp.where(first, a, 0.0), axis=-1, keepdims=True)
        return jnp.where(first, part, total - part)

    s = [state[:, p * LANES:(p + 1) * LANES] for p in range(npairs)]
    for c in range(nck):
        rows = slice(c * L, (c + 1) * L)
        os = []
        for p in range(npairs):
            pl_ = slice(p * LANES, (p + 1) * LANES)
            sb = s[p].astype(BF16)
            os.append(lax.dot_general(rhat_ref[0, rows, pl_], block_diag(sb), nt, preferred_element_type=F32)
                      + ohat_ref[0, rows, pl_])
            s[p] = (s[p] * pt_ref[0, c, :, pl_]
                    + jnp.dot(sb, block_diag(m_ref[0, c, :, pl_]), preferred_element_type=F32)
                    + c_ref[0, c, :, pl_])
        ds = [o - m for o, m in zip(os, [head_sum(o) * (1.0 / N) for o in os])]
        variances = [head_sum(d * d) * (1.0 / N) for d in ds]
        for p, (d, var) in enumerate(zip(ds, variances)):
            pl_ = slice(p * LANES, (p + 1) * LANES)
            o_ref[0, rows, pl_] = (d * lax.rsqrt(var + GN_EPS) * lnw_ref[:, pl_] + lnb_ref[:, pl_]
                                   + bonus_ref[0, rows, pl_])
    for p in range(npairs):
        state[:, p * LANES:(p + 1) * LANES] = s[p]

    @pl.when(pl.program_id(1) == pl.num_programs(1) - 1)
    def _():
        sT_ref[0] = state[...]


def _rwkv_state(rhat, ohat, bonus, m, c, pt, s0, p):
    B, T, D = ohat.shape
    L = min(CHUNK, T)
    ts = min(STATE_TILE, T)
    nck = ts // L
    N = RW_HEAD
    H = D // N
    tile = pl.BlockSpec((1, ts, D), lambda b, t: (b, t, 0))
    sq = pl.BlockSpec((1, nck, N, D), lambda b, t: (b, t, 0, 0))
    st = pl.BlockSpec((1, N, D), lambda b, t: (b, 0, 0))
    s0_vk = jnp.transpose(s0, (0, 2, 1, 3)).reshape(B, N, D)
    o, sT = pl.pallas_call(
        functools.partial(_rwkv_state_kernel, L),
        grid=(B, T // ts),
        in_specs=[tile, tile, tile, sq, sq, pl.BlockSpec((1, nck, 1, D), lambda b, t: (b, t, 0, 0)), st,
                  _const_spec((1, D)), _const_spec((1, D))],
        out_specs=[tile, st],
        out_shape=[jax.ShapeDtypeStruct((B, T, D), F32), jax.ShapeDtypeStruct((B, N, D), F32)],
        scratch_shapes=[pltpu.VMEM((N, D), F32)],
        compiler_params=_params(("parallel", "arbitrary")),
        name="rwkv_state",
    )(rhat, ohat, bonus, m, c, pt, s0_vk, p['ln_w'], p['ln_b'])
    return o, jnp.transpose(sT.reshape(B, N, H, N), (0, 2, 1, 3))


def _rwkv_scan(rt, at, bt, kt, v, pt, s0, p):
    rhat, ohat, bonus, m, c = _rwkv_maps(rt, at, bt, kt, v, pt, p['r_k'])
    return _rwkv_state(rhat, ohat, bonus, m, c, pt, s0, p)


def _post_kernel(x_ref, a_ref, b_ref, wo_ref, gmix_ref, gpre_ref, wgu_ref, wd_ref, gpost_ref, o_ref):
    f = wd_ref.shape[1]
    tm = x_ref.shape[0]
    rows = min(POST_ROWS, tm)
    groups = [slice(i * rows, (i + 1) * rows) for i in range(tm // rows)]
    ys = [_mm(a_ref[r, :] * b_ref[r, :], wo_ref[...]) for r in groups]
    xs = [x_ref[r, :] + _rms(y, gmix_ref[...]) for r, y in zip(groups, ys)]
    gus = [_mm(_rms(x, gpre_ref[...]), wgu_ref[0]) for x in xs]
    acts = [gu[:, :f] * jax.nn.sigmoid(gu[:, :f]) * gu[:, f:] for gu in gus]
    downs = [_mm(act, wd_ref[0]) for act in acts]
    for r, x, down in zip(groups, xs, downs):
        o_ref[r, :] = x + _rms(down, gpost_ref[...])


def _post(x, a, b, wo, lp):
    R, D = x.shape
    tm = min(TOKEN_TILE, R)
    tile = pl.BlockSpec((tm, D), lambda t: (t, 0))
    layer = lp['layer']
    wgu, wd = lp['ffn_w_gu_all'], lp['ffn_w_down_all']
    one_layer = lambda w: pl.BlockSpec((1,) + w.shape[1:], lambda t: (layer, 0, 0), pipeline_mode=pl.Buffered(1))
    vec = _const_spec((1, D))
    return pl.pallas_call(
        _post_kernel,
        grid=(R // tm,),
        in_specs=[tile, tile, tile, _const_spec(wo.shape), vec, vec, one_layer(wgu), one_layer(wd), vec],
        out_specs=tile,
        out_shape=jax.ShapeDtypeStruct((R, D), F32),
        compiler_params=_params(("parallel",)),
        name="post",
    )(x, a, b, wo, lp['g_mix'], lp['g_ffn_pre'], wgu, wd, lp['g_ffn_post'])


def _mlstm_pre_kernel(L, nq, x_ref, g0_ref, win_ref, wgc_ref, wgr_ref, bc_ref, br_ref,
                      q_ref, k_ref, v_ref, og_ref, gc_ref, gr_ref):
    nv = v_ref.shape[1]
    dqk = nq // ML_HEADS
    tm = x_ref.shape[0]
    rows = min(POST_ROWS, tm)
    groups = [slice(i * rows, (i + 1) * rows) for i in range(tm // rows)]
    hs = [_rms(x_ref[r, :], g0_ref[...]).astype(BF16) for r in groups]
    zs = [jnp.dot(hg, win_ref[...], preferred_element_type=F32) for hg in hs]
    for r, z in zip(groups, zs):
        q_ref[r, :] = z[:, :nq] * (dqk ** -0.5)
        k_ref[r, :] = z[:, nq:2 * nq]
        v_ref[r, :] = z[:, 2 * nq:2 * nq + nv]
        og_ref[r, :] = jax.nn.sigmoid(z[:, 2 * nq + nv:])
    h = jnp.concatenate(hs, axis=0) if len(hs) > 1 else hs[0]

    def cap(t):
        return GATE_CAP * jnp.tanh(t / GATE_CAP)

    gc = cap(jnp.dot(h, wgc_ref[...], preferred_element_type=F32) + bc_ref[...])
    lane = lax.broadcasted_iota(jnp.int32, gc.shape, 1)
    gc_ref[...] = jnp.where(lane < ML_HEADS, gc, jax.nn.log_sigmoid(gc))
    gr = cap(lax.dot_general(wgr_ref[...], h, (((1,), (1,)), ((), ())), preferred_element_type=F32)
             + br_ref[...])
    rowi = lax.broadcasted_iota(jnp.int32, gr.shape, 0)
    gr = jnp.where(rowi < ML_HEADS, gr, jax.nn.log_sigmoid(gr))
    for c in range(gr_ref.shape[0]):
        gr_ref[c] = gr[:, c * L:(c + 1) * L]


def _mlstm_pre(x, L, p):
    R, D = x.shape
    tm = min(TOKEN_TILE, R)
    nq = ML_HEADS * (D // (2 * ML_HEADS))
    nv = D
    consts = [p['g0'], p['w_in'], p['w_gc'], p['w_gr'], p['b_c'], p['b_r']]
    row = lambda w: pl.BlockSpec((tm, w), lambda t: (t, 0))
    return pl.pallas_call(
        functools.partial(_mlstm_pre_kernel, L, nq),
        grid=(R // tm,),
        in_specs=[row(D)] + [_const_spec(c.shape) for c in consts],
        out_specs=[row(nq), row(nq), row(nv), row(nv), row(LANES),
                   pl.BlockSpec((tm // L, GATE_ROWS, L), lambda t: (t, 0, 0))],
        out_shape=[jax.ShapeDtypeStruct((R, nq), F32), jax.ShapeDtypeStruct((R, nq), F32),
                   jax.ShapeDtypeStruct((R, nv), F32), jax.ShapeDtypeStruct((R, nv), F32),
                   jax.ShapeDtypeStruct((R, LANES), F32),
                   jax.ShapeDtypeStruct((R // L, GATE_ROWS, L), F32)],
        compiler_params=_params(("parallel",)),
        name="mlstm_pre",
    )(x, *consts)


def _mlstm_scan_kernel(L, q_ref, k_ref, v_ref, gc_ref, gr_ref, c0_ref, n0_ref, m0_ref, hnw_ref,
                       h_ref, cT_ref, nT_ref, mT_ref, c_st, n_st, m_st):
    H = ML_HEADS
    nck = q_ref.shape[0] // L
    dqk = q_ref.shape[1] // H
    dv = v_ref.shape[1] // H

    @pl.when(pl.program_id(1) == 0)
    def _():
        c_st[...] = c0_ref[0]
        n_st[...] = n0_ref[0]
        m_st[...] = m0_ref[0]

    row = lax.broadcasted_iota(jnp.int32, (L, L), 0)
    col = lax.broadcasted_iota(jnp.int32, (L, L), 1)
    causal = row >= col
    tril = causal.astype(BF16)
    triu = (row <= col).astype(BF16)
    nt = (((1,), (1,)), ((), ()))
    tn = (((0,), (0,)), ((), ()))

    gcs = [gc_ref[c * L:(c + 1) * L, :] for c in range(nck)]
    grs = [gr_ref[c] for c in range(nck)]
    bcs = [_dot_exact_lhs(tril, gc) for gc in gcs]
    brs = []
    for gr in grs:
        hi, mid, lo = _split3(gr)
        brs.append(jnp.dot(hi, triu, preferred_element_type=F32) + jnp.dot(mid, triu, preferred_element_type=F32)
                   + jnp.dot(lo, triu, preferred_element_type=F32))

    where = [(c, hd) for c in range(nck) for hd in range(H)]
    rows_of = [slice(c * L, (c + 1) * L) for c, _ in where]
    qs = [q_ref[rows, hd * dqk:(hd + 1) * dqk] for rows, (_, hd) in zip(rows_of, where)]
    ks = [k_ref[rows, hd * dqk:(hd + 1) * dqk] for rows, (_, hd) in zip(rows_of, where)]
    vbs = [v_ref[rows, hd * dv:(hd + 1) * dv].astype(BF16) for rows, (_, hd) in zip(rows_of, where)]
    li_cs = [jnp.broadcast_to(gcs[c][:, hd:hd + 1], (L, dqk)) for c, hd in where]
    b_cs = [jnp.broadcast_to(bcs[c][:, H + hd:H + hd + 1], (L, dqk)) for c, hd in where]
    dmats = [jnp.where(causal, b_c[:, :L] - brs[c][H + hd:H + hd + 1, :] + grs[c][hd:hd + 1, :], -jnp.inf)
             for (c, hd), b_c in zip(where, b_cs)]
    intras = [jnp.broadcast_to(jnp.max(dmat, axis=-1, keepdims=True), (L, dqk)) for dmat in dmats]
    gs = [intra - b_c for intra, b_c in zip(intras, b_cs)]
    g_ends = [g[L - 1:L, :] for g in gs]
    g_lasts = [g_end[:, :1] for g_end in g_ends]
    b_lasts = [b_c[L - 1:L, :1] for b_c in b_cs]
    es = [jnp.exp(dmat - intra[:, :L]) for dmat, intra in zip(dmats, intras)]
    kws = [k * jnp.exp(li_c - b_c - g_end) for k, li_c, b_c, g_end in zip(ks, li_cs, b_cs, g_ends)]
    qbs = [q.astype(BF16) for q in qs]
    qks = [lax.dot_general(qb, k.astype(BF16), nt, preferred_element_type=F32) * e
           for qb, k, e in zip(qbs, ks, es)]
    num_i = [_mm(qk, vb) for qk, vb in zip(qks, vbs)]
    kvs = [lax.dot_general(kw.astype(BF16), vb, tn, preferred_element_type=F32) for kw, vb in zip(kws, vbs)]
    ksums = [jnp.sum(kw, axis=0, keepdims=True) for kw in kws]
    qk_sums = [jnp.sum(qk, axis=-1, keepdims=True) for qk in qks]

    c_cur = [c_st[hd] for hd in range(H)]
    n_cur = [n_st[hd] for hd in range(H)]
    m_cur = [m_st[hd] for hd in range(H)]
    c_prevs, n_prevs, m_prevs = [], [], []
    for (c, hd), g_last, b_last, kv, ksum in zip(where, g_lasts, b_lasts, kvs, ksums):
        c_prevs.append(c_cur[hd].astype(BF16))
        n_prevs.append(n_cur[hd])
        m_prevs.append(m_cur[hd])
        mx = jnp.maximum(m_cur[hd], g_last)
        e_old = jnp.exp(m_cur[hd] - mx)
        e_new = jnp.exp(g_last - mx)
        c_cur[hd] = e_old * c_cur[hd] + e_new * kv
        n_cur[hd] = e_old * n_cur[hd] + e_new * ksum
        m_cur[hd] = b_last + mx
    for hd in range(H):
        c_st[hd] = c_cur[hd]
        n_st[hd] = n_cur[hd]
        m_st[hd] = m_cur[hd]

    qcs = [_mm(qb, cp) for qb, cp in zip(qbs, c_prevs)]
    qn_sums = [jnp.sum(q * n_prev, axis=-1, keepdims=True) for q, n_prev in zip(qs, n_prevs)]
    mxs = [jnp.maximum(m_prev, g) for g, m_prev in zip(gs, m_prevs)]
    f_news = [jnp.exp(g - mx) for g, mx in zip(gs, mxs)]
    f_olds = [jnp.exp(m_prev - mx) for mx, m_prev in zip(mxs, m_prevs)]
    floors = [jnp.exp(-(b_c + mx)) for b_c, mx in zip(b_cs, mxs)]
    invs = [1.0 / jnp.maximum(jnp.abs(f_new * qk_sum + f_old * qn_sum), floor)
            for f_new, f_old, qk_sum, qn_sum, floor in zip(f_news, f_olds, qk_sums, qn_sums, floors)]
    wide = lambda a: jnp.concatenate([a] * (dv // dqk), axis=1)
    hs = [wide(f_new * inv) * ni + wide(f_old * inv) * qc
          for f_new, f_old, inv, ni, qc in zip(f_news, f_olds, invs, num_i, qcs)]
    h_ms = [jnp.mean(h * h, axis=-1, keepdims=True) for h in hs]
    for rows, (_, hd), h, h_m in zip(rows_of, where, hs, h_ms):
        h_ref[rows, hd * dv:(hd + 1) * dv] = h * lax.rsqrt(h_m + EPS) * hnw_ref[:, hd * dv:(hd + 1) * dv]

    @pl.when(pl.program_id(1) == pl.num_programs(1) - 1)
    def _():
        cT_ref[0] = c_st[...]
        nT_ref[0] = n_st[...]
        mT_ref[0] = m_st[...]


def _mlstm_scan(q, k, v, gc, gr, c0, n0, m0, hn_w, B, T):
    L = min(CHUNK, T)
    ts = min(MLSTM_TILE, T)
    nc = T // ts
    H = ML_HEADS
    nq, nv = q.shape[1], v.shape[1]
    dqk, dv = nq // H, nv // H
    row = lambda w: pl.BlockSpec((ts, w), lambda b, c: (b * nc + c, 0))
    st = lambda *s: pl.BlockSpec((1,) + s, lambda b, c: (b,) + (0,) * len(s))
    return pl.pallas_call(
        functools.partial(_mlstm_scan_kernel, L),
        grid=(B, nc),
        in_specs=[row(nq), row(nq), row(nv), row(LANES),
                  pl.BlockSpec((ts // L, GATE_ROWS, L), lambda b, c: (b * nc + c, 0, 0)),
                  st(H, dqk, dv), st(H, 1, dqk), st(H, 1, 1), _const_spec(hn_w.shape)],
        out_specs=[row(nv), st(H, dqk, dv), st(H, 1, dqk), st(H, 1, 1)],
        out_shape=[jax.ShapeDtypeStruct((B * T, nv), F32), jax.ShapeDtypeStruct((B, H, dqk, dv), F32),
                   jax.ShapeDtypeStruct((B, H, 1, dqk), F32), jax.ShapeDtypeStruct((B, H, 1, 1), F32)],
        scratch_shapes=[pltpu.VMEM((H, dqk, dv), F32), pltpu.VMEM((H, 1, dqk), F32),
                        pltpu.VMEM((H, 1, 1), F32)],
        compiler_params=_params(("parallel", "arbitrary")),
        name="mlstm_scan",
    )(q, k, v, gc, gr, c0, n0.reshape(B, H, 1, dqk), m0.reshape(B, H, 1, 1), hn_w)


def _prep_weights(w):
    D = w['norm_g'].shape[-1]
    H = D // RW_HEAD
    row = lambda a: a.reshape(1, -1).astype(F32)
    head_of_lane = jnp.arange(D) // RW_HEAD
    hsum = (head_of_lane[:, None] == jnp.arange(LANES)[None, :]).astype(BF16)
    hexp = hsum.T
    ffn_w_gu_all = w['ffn_w_gu'].astype(BF16)
    ffn_w_down_all = w['ffn_w_down'].astype(BF16)
    layers = []
    for i in range(DEPTH):
        j = i // N_MIXERS
        ng = w['norm_g'][i]
        lp = {'g0': row(ng[0]), 'g_mix': row(ng[1]), 'g_ffn_pre': row(ng[2]), 'g_ffn_post': row(ng[3]),
              'layer': i, 'ffn_w_gu_all': ffn_w_gu_all, 'ffn_w_down_all': ffn_w_down_all}
        if i % N_MIXERS == 0:
            lp.update({
                'mu': jnp.pad(w['rw_mu'][j], ((0, 2), (0, 0))),
                'w_rkv': w['rw_w_rkv'][j].astype(BF16),
                'w0': row(w['rw_w0'][j]), 'w1': w['rw_w1'][j].astype(BF16), 'w2': w['rw_w2'][j].astype(BF16),
                'a0': row(w['rw_a0'][j]), 'a1': w['rw_a1'][j].astype(BF16), 'a2': w['rw_a2'][j].astype(BF16),
                'g1': w['rw_g1'][j].astype(BF16), 'g2': w['rw_g2'][j].astype(BF16),
                'k_k': row(w['rw_k_k'][j]), 'k_a': row(w['rw_k_a'][j]), 'r_k': row(w['rw_r_k'][j]),
                'ln_w': row(w['rw_ln_w'][j]), 'ln_b': row(w['rw_ln_b'][j]),
                'w_o': w['rw_w_o'][j].astype(BF16), 'hsum': hsum, 'hexp': hexp,
            })
            if j > 0:
                lp.update({'v0': row(w['rw_v0'][j - 1]), 'v1': w['rw_v1'][j - 1].astype(BF16),
                           'v2': w['rw_v2'][j - 1].astype(BF16)})
        else:
            w_in = w['ml_w_in'][j]
            n_main = w_in.shape[1] - 2 * ML_HEADS
            w_g = w_in[:, n_main:]
            bias = w['ml_b_gates'][j].astype(F32)
            lp.update({
                'w_in': w_in[:, :n_main].astype(BF16),
                'w_gc': jnp.pad(w_g, ((0, 0), (0, LANES - 2 * ML_HEADS))).astype(BF16),
                'w_gr': jnp.pad(w_g.T, ((0, GATE_ROWS - 2 * ML_HEADS), (0, 0))).astype(BF16),
                'b_c': jnp.pad(bias, (0, LANES - 2 * ML_HEADS)).reshape(1, LANES),
                'b_r': jnp.pad(bias, (0, GATE_ROWS - 2 * ML_HEADS)).reshape(GATE_ROWS, 1),
                'hn_w': row(w['ml_hn_w'][j]), 'w_out': w['ml_w_out'][j].astype(BF16),
            })
        layers.append(lp)
    return layers


def _trunk(x, rw_shift, rw_S, ml_C, ml_n, ml_m, layers):
    B, T, D = x.shape
    L = min(CHUNK, T)
    flat = lambda a: a.reshape(B * T, a.shape[-1])
    v_first = None
    shifts, Ss, Cs, ns, ms = [], [], [], [], []
    xf = flat(x)
    for i, lp in enumerate(layers):
        j = i // N_MIXERS
        if i % N_MIXERS == 0:
            rt, at, bt, kt, v, g, pt, sh = _rwkv_pre(xf.reshape(B, T, D), rw_shift[j], lp, j, v_first)
            if j == 0:
                v_first = v
            o, S = _rwkv_scan(rt, at, bt, kt, v, pt, rw_S[j], lp)
            shifts.append(sh)
            Ss.append(S)
            xf = _post(xf, flat(o), flat(g), lp['w_o'], lp)
        else:
            q, k, v, og, gc, gr = _mlstm_pre(xf, L, lp)
            hn, C, n, m = _mlstm_scan(q, k, v, gc, gr, ml_C[j], ml_n[j], ml_m[j], lp['hn_w'], B, T)
            Cs.append(C)
            ns.append(n.reshape(B, ML_HEADS, -1))
            ms.append(m.reshape(B, ML_HEADS))
            xf = _post(xf, og, hn, lp['w_out'], lp)
    return xf.reshape(B, T, D), jnp.stack(shifts), jnp.stack(Ss), jnp.stack(Cs), jnp.stack(ns), jnp.stack(ms)


def kernel(x_prompt, x_sample, state_rwkv_shift, state_rwkv_S, state_mlstm_C, state_mlstm_n, state_mlstm_m,
           norm_g, rw_mu, rw_w_rkv, rw_w0, rw_w1, rw_w2, rw_a0, rw_a1, rw_a2, rw_v0, rw_v1, rw_v2,
           rw_g1, rw_g2, rw_k_k, rw_k_a, rw_r_k, rw_ln_w, rw_ln_b, rw_w_o,
           ml_w_in, ml_b_gates, ml_hn_w, ml_w_out, ffn_w_gu, ffn_w_down):
    w = {
        'norm_g': norm_g, 'rw_mu': rw_mu, 'rw_w_rkv': rw_w_rkv, 'rw_w0': rw_w0, 'rw_w1': rw_w1,
        'rw_w2': rw_w2, 'rw_a0': rw_a0, 'rw_a1': rw_a1, 'rw_a2': rw_a2, 'rw_v0': rw_v0, 'rw_v1': rw_v1,
        'rw_v2': rw_v2, 'rw_g1': rw_g1, 'rw_g2': rw_g2, 'rw_k_k': rw_k_k, 'rw_k_a': rw_k_a,
        'rw_r_k': rw_r_k, 'rw_ln_w': rw_ln_w, 'rw_ln_b': rw_ln_b, 'rw_w_o': rw_w_o,
        'ml_w_in': ml_w_in, 'ml_b_gates': ml_b_gates, 'ml_hn_w': ml_hn_w, 'ml_w_out': ml_w_out,
        'ffn_w_gu': ffn_w_gu, 'ffn_w_down': ffn_w_down,
    }
    layers = _prep_weights(w)
    Bp = x_prompt.shape[0]
    zeros = lambda a: jnp.zeros((a.shape[0], Bp) + a.shape[2:], F32)
    out_p = _trunk(x_prompt, zeros(state_rwkv_shift), zeros(state_rwkv_S), zeros(state_mlstm_C),
                   zeros(state_mlstm_n), zeros(state_mlstm_m), layers)
    out_s = _trunk(x_sample, state_rwkv_shift, state_rwkv_S, state_mlstm_C, state_mlstm_n, state_mlstm_m,
                   layers)
    return (out_p[0], out_s[0]) + tuple(out_p[1:]) + tuple(out_s[1:])
```
